```python
import jax, jax.numpy as jnp
from jax import lax
import numpy as np

D_MODEL = 4096
BATCH = 4
SEQ = 4096
DEPTH = 1

MIX_WIDTH = D_MODEL
RET_HEADS = 8
RET_WIDTH = MIX_WIDTH // 2
RET_HEAD_DIM = RET_WIDTH // RET_HEADS
SGU_GROUPS = 8
SGU_WIDTH = MIX_WIDTH - RET_WIDTH
SGU_GROUP_DIM = SGU_WIDTH // SGU_GROUPS
CHUNK = 128
IN_WIDTH = 4 * RET_WIDTH + 2 * SGU_WIDTH
D_FF = 256 * ((8 * D_MODEL // 3 + 255) // 256)
ROPE_BASE = 10000.0
EPS = 1e-6
N_MOD = 9

kernel_name = "hybrid_retention_gmlp_macaron_adaln"


def rmsnorm(x, g):
    xf = x.astype(jnp.float32)
    y = xf * lax.rsqrt(jnp.mean(xf * xf, axis=-1, keepdims=True) + EPS)
    return (y * g.astype(jnp.float32)).astype(x.dtype)


def modulate(h, shift, scale):
    return h * (1.0 + scale[:, None, :]) + shift[:, None, :]


def swiglu(h, w1, w3, w2):
    return (jax.nn.silu(h @ w1) * (h @ w3)) @ w2


def rotary(t, positions):
    dh = t.shape[-1]
    half = dh // 2
    inv_freq = ROPE_BASE ** (-jnp.arange(0, half, dtype=jnp.float32) / half)
    ang = positions.astype(jnp.float32)[..., None] * inv_freq
    cos = jnp.cos(ang)[:, :, None, :]
    sin = jnp.sin(ang)[:, :, None, :]
    tf = t.astype(jnp.float32)
    t1, t2 = tf[..., :half], tf[..., half:]
    return jnp.concatenate([t1 * cos - t2 * sin, t1 * sin + t2 * cos], axis=-1)


def retention_chunkwise(q, k, v):
    b, s, h, dh = q.shape
    nc = s // CHUNK
    log_gamma = jnp.log(1.0 - 2.0 ** (-5.0 - jnp.arange(h, dtype=jnp.float32)))
    idx = jnp.arange(CHUNK)
    dist = (idx[:, None] - idx[None, :]).astype(jnp.float32)
    intra_decay = jnp.where(dist[None] >= 0,
                            jnp.exp(log_gamma[:, None, None] * jnp.maximum(dist, 0.0)[None]), 0.0)
    xi = jnp.exp(log_gamma[:, None] * (idx + 1).astype(jnp.float32))[None, :, :, None]
    zeta = jnp.exp(log_gamma[:, None] * (CHUNK - 1 - idx).astype(jnp.float32))[None, :, :, None]
    chunk_decay = jnp.exp(log_gamma * CHUNK)[None, :, None, None]

    def to_chunks(t):
        return t.reshape(b, nc, CHUNK, h, dh).transpose(1, 0, 3, 2, 4)

    qc, kc, vc = to_chunks(q), to_chunks(k * (dh ** -0.5)), to_chunks(v)

    def step(state, inp):
        qi, ki, vi = inp
        scores = jnp.einsum('bhid,bhjd->bhij', qi, ki) * intra_decay[None]
        intra = jnp.einsum('bhij,bhje->bhie', scores, vi)
        inter = jnp.einsum('bhid,bhde->bhie', qi, state) * xi
        new_state = state * chunk_decay + jnp.einsum('bhjd,bhje->bhde', ki * zeta, vi)
        return new_state, intra + inter

    state0 = jnp.zeros((b, h, dh, dh), jnp.float32)
    _, out = lax.scan(step, state0, (qc, kc, vc))
    return out.transpose(1, 0, 3, 2, 4).reshape(b, s, h, dh)


def chunked_spatial_gating(u, vs, norm_g, norm_b, w_s, b_s):
    b, s, _ = u.shape
    nc = s // CHUNK
    vg = vs.reshape(b, s, SGU_GROUPS, SGU_GROUP_DIM).astype(jnp.float32)
    mu = jnp.mean(vg, axis=-1, keepdims=True)
    var = jnp.mean(jnp.square(vg - mu), axis=-1, keepdims=True)
    vg = (vg - mu) * lax.rsqrt(var + EPS) * norm_g + norm_b
    vg = vg.reshape(b, nc, CHUNK, SGU_GROUPS, SGU_GROUP_DIM)
    causal = jnp.tril(jnp.ones((CHUNK, CHUNK), jnp.float32))
    w_masked = w_s.astype(jnp.float32) * causal[None]
    mixed = jnp.einsum('gij,bnjgd->bnigd', w_masked, vg) + b_s.T.astype(jnp.float32)[None, None, :, :, None]
    gate = mixed.reshape(b, s, SGU_WIDTH).astype(u.dtype)
    return u * gate


def setup_inputs(seed: int = 0) -> dict:
    key = jax.random.key(seed)
    ks = jax.random.split(key, 24)
    L, D = DEPTH, D_MODEL
    nrm = lambda k, shape, fan_in: jax.random.normal(k, shape, jnp.float32) * (fan_in ** -0.5)
    gain = lambda k, shape: 1.0 + 0.02 * jax.random.normal(k, shape, jnp.float32)
    x = jax.random.normal(ks[0], (BATCH, SEQ, D), jnp.float32)
    c = jax.random.normal(ks[1], (BATCH, D), jnp.float32)
    offsets = jax.random.randint(ks[2], (BATCH, 1), 0, SEQ, dtype=jnp.int32)
    positions = (jnp.arange(SEQ, dtype=jnp.int32)[None, :] + offsets).astype(jnp.int32)
    return {
        "x": x,
        "c": c,
        "positions": positions,
        "ada_w": nrm(ks[3], (L, D, N_MOD * D), D),
        "ada_b": 0.02 * jax.random.normal(ks[4], (L, N_MOD * D), jnp.float32),
        "norm_ffn1_g": gain(ks[5], (L, D)),
        "ffn1_w1": nrm(ks[6], (L, D, D_FF), D),
        "ffn1_w3": nrm(ks[7], (L, D, D_FF), D),
        "ffn1_w2": nrm(ks[8], (L, D_FF, D), D_FF),
        "norm_mix_g": gain(ks[9], (L, D)),
        "w_in": nrm(ks[10], (L, D, IN_WIDTH), D),
        "sgu_norm_g": gain(ks[11], (L, SGU_GROUPS, SGU_GROUP_DIM)),
        "sgu_norm_b": 0.02 * jax.random.normal(ks[12], (L, SGU_GROUPS, SGU_GROUP_DIM), jnp.float32),
        "sgu_w_s": nrm(ks[13], (L, SGU_GROUPS, CHUNK, CHUNK), CHUNK),
        "sgu_b_s": gain(ks[14], (L, SGU_GROUPS, CHUNK)),
        "w_out": nrm(ks[15], (L, MIX_WIDTH, D), MIX_WIDTH),
        "norm_ffn2_g": gain(ks[16], (L, D)),
        "ffn2_w1": nrm(ks[17], (L, D, D_FF), D),
        "ffn2_w3": nrm(ks[18], (L, D, D_FF), D),
        "ffn2_w2": nrm(ks[19], (L, D_FF, D), D_FF),
        "final_norm_g": gain(ks[20], (D,)),
    }


def reference(x, c, positions, ada_w, ada_b, norm_ffn1_g, ffn1_w1, ffn1_w3, ffn1_w2, norm_mix_g, w_in,
              sgu_norm_g, sgu_norm_b, sgu_w_s, sgu_b_s, w_out, norm_ffn2_g, ffn2_w1, ffn2_w3, ffn2_w2,
              final_norm_g):
    b, s, _ = x.shape
    for l in range(DEPTH):
        mod = jax.nn.silu(c) @ ada_w[l] + ada_b[l]
        (sh1, sc1, gt1, sh2, sc2, gt2, sh3, sc3, gt3) = jnp.split(mod, N_MOD, axis=-1)

        h = modulate(rmsnorm(x, norm_ffn1_g[l]), sh1, sc1)
        x = x + 0.5 * gt1[:, None, :] * swiglu(h, ffn1_w1[l], ffn1_w3[l], ffn1_w2[l])

        h = modulate(rmsnorm(x, norm_mix_g[l]), sh2, sc2)
        proj = h @ w_in[l]
        q, k, v, g, u, vs = jnp.split(proj, [RET_WIDTH, 2 * RET_WIDTH, 3 * RET_WIDTH, 4 * RET_WIDTH,
                                             4 * RET_WIDTH + SGU_WIDTH], axis=-1)
        hs = (b, s, RET_HEADS, RET_HEAD_DIM)
        qr = rotary(q.reshape(hs), positions)
        kr = rotary(k.reshape(hs), positions)
        ret = retention_chunkwise(qr, kr, v.reshape(hs).astype(jnp.float32))
        ret = ret * lax.rsqrt(jnp.mean(ret * ret, axis=-1, keepdims=True) + EPS)
        ret = ret.reshape(b, s, RET_WIDTH).astype(x.dtype) * jax.nn.silu(g)
        sgu = chunked_spatial_gating(jax.nn.gelu(u), jax.nn.gelu(vs), sgu_norm_g[l], sgu_norm_b[l],
                                     sgu_w_s[l], sgu_b_s[l])
        mixed = jnp.concatenate([ret, sgu], axis=-1) @ w_out[l]
        x = x + gt2[:, None, :] * mixed

        h = modulate(rmsnorm(x, norm_ffn2_g[l]), sh3, sc3)
        x = x + 0.5 * gt3[:, None, :] * swiglu(h, ffn2_w1[l], ffn2_w3[l], ffn2_w2[l])
    return rmsnorm(x, final_norm_g)
```

```python
import functools

import jax
import jax.numpy as jnp
from jax import lax
from jax.experimental import pallas as pl
from jax.experimental.pallas import tpu as pltpu

F32 = jnp.float32
BF16 = jnp.bfloat16

RET_HEADS = 8
SGU_GROUPS = 8
CHUNK = 128
ROPE_BASE = 10000.0
EPS = 1e-6
N_MOD = 9

V7X_VMEM_BYTES = 64 * 1024 * 1024
VMEM_LIMIT_BYTES = V7X_VMEM_BYTES - 8 * 1024 * 1024


def _params(*semantics):
    return pltpu.CompilerParams(dimension_semantics=semantics, vmem_limit_bytes=VMEM_LIMIT_BYTES)


def _silu(x):
    return x / (1.0 + jnp.exp(-x))


def _gelu_tanh(x):
    return 0.5 * x * (1.0 + jnp.tanh(0.7978845608028654 * (x + 0.044715 * (x * x * x))))


def _ada_kernel(c_ref, w_ref, b_ref, o_ref):
    sc = _silu(c_ref[...]).astype(BF16)
    o_ref[...] = jnp.dot(sc, w_ref[...].astype(BF16), preferred_element_type=F32) + b_ref[...]


def _ada_mod(c_pad, w, b, tn=512):
    rows, d = c_pad.shape
    n = w.shape[1]
    return pl.pallas_call(
        _ada_kernel,
        grid=(n // tn,),
        in_specs=[
            pl.BlockSpec((rows, d), lambda j: (0, 0)),
            pl.BlockSpec((d, tn), lambda j: (0, j)),
            pl.BlockSpec((1, tn), lambda j: (0, j)),
        ],
        out_specs=pl.BlockSpec((rows, tn), lambda j: (0, j)),
        out_shape=jax.ShapeDtypeStruct((rows, n), F32),
        compiler_params=_params("parallel"),
        name="ada_mod",
    )(c_pad, w, b.reshape(1, n))


def _rms(x, g):
    return x * lax.rsqrt(jnp.mean(x * x, axis=-1, keepdims=True) + EPS) * g


def _norm_mod_kernel(x_ref, g_ref, sh_ref, sc_ref, o_ref):
    y = _rms(x_ref[...], g_ref[...])
    o_ref[...] = (y * (1.0 + sc_ref[0]) + sh_ref[0]).astype(o_ref.dtype)


def _norm_kernel(x_ref, g_ref, o_ref):
    o_ref[...] = _rms(x_ref[...], g_ref[...]).astype(o_ref.dtype)


def _norm_mod(x2, g, mod3, seg_shift, seg_scale, seq, tm=256):
    m, d = x2.shape
    bidx = lambda i: (i * tm) // seq
    return pl.pallas_call(
        _norm_mod_kernel,
        grid=(m // tm,),
        in_specs=[
            pl.BlockSpec((tm, d), lambda i: (i, 0)),
            pl.BlockSpec((1, d), lambda i: (0, 0)),
            pl.BlockSpec((1, 1, d), lambda i: (bidx(i), 0, seg_shift)),
            pl.BlockSpec((1, 1, d), lambda i: (bidx(i), 0, seg_scale)),
        ],
        out_specs=pl.BlockSpec((tm, d), lambda i: (i, 0)),
        out_shape=jax.ShapeDtypeStruct((m, d), BF16),
        compiler_params=_params("parallel"),
        name="norm_mod",
    )(x2, g.reshape(1, d), mod3, mod3)


def _final_norm(x2, g, tm=256):
    m, d = x2.shape
    return pl.pallas_call(
        _norm_kernel,
        grid=(m // tm,),
        in_specs=[pl.BlockSpec((tm, d), lambda i: (i, 0)), pl.BlockSpec((1, d), lambda i: (0, 0))],
        out_specs=pl.BlockSpec((tm, d), lambda i: (i, 0)),
        out_shape=jax.ShapeDtypeStruct((m, d), F32),
        compiler_params=_params("parallel"),
        name="final_norm",
    )(x2, g.reshape(1, d))


def _rope_kernel(pos_ref, freq_ref, cos_ref, sin_ref):
    ang = pos_ref[...].astype(F32) * freq_ref[...]
    cos_ref[...] = jnp.cos(ang)
    sin_ref[...] = jnp.sin(ang)


def _rope_tables(pos_col, inv_freq, tm=1024):
    m = pos_col.shape[0]
    half = inv_freq.shape[0]
    spec = pl.BlockSpec((tm, half), lambda i: (i, 0))
    return pl.pallas_call(
        _rope_kernel,
        grid=(m // tm,),
        in_specs=[pl.BlockSpec((tm, 1), lambda i: (i, 0)), pl.BlockSpec((1, half), lambda i: (0, 0))],
        out_specs=[spec, spec],
        out_shape=[jax.ShapeDtypeStruct((m, half), F32)] * 2,
        compiler_params=_params("parallel"),
        name="rope_tables",
    )(pos_col, inv_freq.reshape(1, half))


def _ffn_up_kernel(h_ref, w1_ref, w3_ref, o_ref):
    h = h_ref[...]
    a = jnp.dot(h, w1_ref[...], preferred_element_type=F32)
    b = jnp.dot(h, w3_ref[...], preferred_element_type=F32)
    o_ref[...] = (_silu(a) * b).astype(o_ref.dtype)


def _ffn_up(h, w1, w3, tm=2048, tn=256):
    m, k = h.shape
    f = w1.shape[1]
    wspec = pl.BlockSpec((k, tn), lambda i, j: (0, j))
    return pl.pallas_call(
        _ffn_up_kernel,
        grid=(m // tm, f // tn),
        in_specs=[pl.BlockSpec((tm, k), lambda i, j: (i, 0)), wspec, wspec],
        out_specs=pl.BlockSpec((tm, tn), lambda i, j: (i, j)),
        out_shape=jax.ShapeDtypeStruct((m, f), BF16),
        compiler_params=_params("parallel", "parallel"),
        name="ffn_up",
    )(h, w1, w3)


def _residual_kernel(*refs, n_pairs, scale):
    a_refs = refs[:n_pairs]
    b_refs = refs[n_pairs:2 * n_pairs]
    x_ref, gate_ref, o_ref = refs[2 * n_pairs:]
    acc = jnp.dot(a_refs[0][...], b_refs[0][...], preferred_element_type=F32)
    for a_ref, b_ref in zip(a_refs[1:], b_refs[1:]):
        acc = acc + jnp.dot(a_ref[...], b_ref[...], preferred_element_type=F32)
    o_ref[...] = x_ref[...] + (scale * gate_ref[0]) * acc


def _matmul_residual(a_list, b_list, x2, mod3, seg_gate, scale, seq, tm, tn):
    m, n = x2.shape
    n_pairs = len(a_list)
    in_specs = []
    for a in a_list:
        in_specs.append(pl.BlockSpec((tm, a.shape[1]), lambda i, j: (i, 0)))
    operands = list(a_list)
    for w, row_block, rows in b_list:
        in_specs.append(pl.BlockSpec((rows, tn), lambda i, j, rb=row_block: (rb, j)))
        operands.append(w)
    in_specs.append(pl.BlockSpec((tm, tn), lambda i, j: (i, j)))
    in_specs.append(pl.BlockSpec((1, 1, tn), lambda i, j: ((i * tm) // seq, 0, seg_gate * (n // tn) + j)))
    return pl.pallas_call(
        functools.partial(_residual_kernel, n_pairs=n_pairs, scale=scale),
        grid=(m // tm, n // tn),
        in_specs=in_specs,
        out_specs=pl.BlockSpec((tm, tn), lambda i, j: (i, j)),
        out_shape=jax.ShapeDtypeStruct((m, n), F32),
        compiler_params=_params("parallel", "parallel"),
        name="matmul_residual",
    )(*operands, x2, mod3)


def _proj_rotary_kernel(h_ref, w_ref, cos_ref, sin_ref, o_ref, *, head_dim, k_blocks_from, k_scale):
    acc = jnp.dot(h_ref[...], w_ref[...], preferred_element_type=F32)
    scale = jnp.where(pl.program_id(1) >= k_blocks_from, k_scale, 1.0).astype(F32)
    cos = cos_ref[...] * scale
    sin = sin_ref[...] * scale
    half = head_dim // 2
    for hd in range(acc.shape[1] // head_dim):
        t1 = acc[:, hd * head_dim:hd * head_dim + half]
        t2 = acc[:, hd * head_dim + half:(hd + 1) * head_dim]
        o_ref[:, hd * head_dim:hd * head_dim + half] = (t1 * cos - t2 * sin).astype(o_ref.dtype)
        o_ref[:, hd * head_dim + half:(hd + 1) * head_dim] = (t1 * sin + t2 * cos).astype(o_ref.dtype)


def _proj_act_kernel(h_ref, w_ref, o_ref, *, act):
    acc = jnp.dot(h_ref[...], w_ref[...], preferred_element_type=F32)
    o_ref[...] = act(acc).astype(o_ref.dtype)


def _proj_gelu_groupnorm_kernel(h_ref, w_ref, g_ref, b_ref, o_ref, *, group_dim):
    acc = jnp.dot(h_ref[...], w_ref[...], preferred_element_type=F32)
    for gi in range(acc.shape[1] // group_dim):
        cols = slice(gi * group_dim, (gi + 1) * group_dim)
        v = _gelu_tanh(acc[:, cols])
        mu = jnp.mean(v, axis=-1, keepdims=True)
        cen = v - mu
        var = jnp.mean(cen * cen, axis=-1, keepdims=True)
        o_ref[:, cols] = (cen * lax.rsqrt(var + EPS) * g_ref[:, cols] + b_ref[:, cols]).astype(o_ref.dtype)


def _proj(kernel, h, w, col0, n, extra, extra_specs, tm=1024, tn=1024):
    m, k = h.shape
    cb0 = col0 // tn
    return pl.pallas_call(
        kernel,
        grid=(m // tm, n // tn),
        in_specs=[pl.BlockSpec((tm, k), lambda i, j: (i, 0)),
                  pl.BlockSpec((k, tn), lambda i, j: (0, cb0 + j))] + extra_specs,
        out_specs=pl.BlockSpec((tm, tn), lambda i, j: (i, j)),
        out_shape=jax.ShapeDtypeStruct((m, n), BF16),
        compiler_params=_params("parallel", "parallel"),
        name="proj",
    )(h, w, *extra)


def _retention_kernel(q_ref, k_ref, v_ref, g_ref, decay_ref, xi_ref, zeta_ref, cd_ref, o_ref, state_ref,
                      *, n_chunks):
    @pl.when(pl.program_id(2) == 0)
    def _():
        state_ref[...] = jnp.zeros_like(state_ref)

    decay = decay_ref[0]
    xi = xi_ref[0]
    zeta = zeta_ref[0]
    cd = cd_ref[0]
    for c in range(n_chunks):
        rows = pl.ds(c * CHUNK, CHUNK)
        q = q_ref[rows, :]
        k = k_ref[rows, :]
        v = v_ref[rows, :]
        state = state_ref[...]
        scores = lax.dot_general(q, k, (((1,), (1,)), ((), ())), preferred_element_type=F32) * decay
        intra = jnp.dot(scores.astype(BF16), v, preferred_element_type=F32)
        inter = jnp.dot(q, state.astype(BF16), preferred_element_type=F32) * xi
        kz = (k.astype(F32) * zeta).astype(BF16)
        kv = lax.dot_general(kz, v, (((0,), (0,)), ((), ())), preferred_element_type=F32)
        state_ref[...] = state * cd + kv
        out = intra + inter
        out = out * lax.rsqrt(jnp.mean(out * out, axis=-1, keepdims=True) + EPS)
        o_ref[rows, :] = (out * g_ref[rows, :].astype(F32)).astype(o_ref.dtype)


def _retention(qk, v, sg, decay, xi, zeta, cd, batch, seq, n_chunks=4):
    m, width = v.shape
    dh = width // RET_HEADS
    rows = n_chunks * CHUNK
    steps = seq // rows
    row_block = lambda b, h, c: b * steps + c
    tok = lambda col0: pl.BlockSpec((rows, dh), lambda b, h, c: (row_block(b, h, c), col0 + h))
    return pl.pallas_call(
        functools.partial(_retention_kernel, n_chunks=n_chunks),
        grid=(batch, RET_HEADS, steps),
        in_specs=[
            tok(0), tok(RET_HEADS), tok(0), tok(0),
            pl.BlockSpec((1, CHUNK, CHUNK), lambda b, h, c: (h, 0, 0)),
            pl.BlockSpec((1, CHUNK, dh), lambda b, h, c: (h, 0, 0)),
            pl.BlockSpec((1, CHUNK, dh), lambda b, h, c: (h, 0, 0)),
            pl.BlockSpec((1, 1, dh), lambda b, h, c: (h, 0, 0)),
        ],
        out_specs=tok(0),
        out_shape=jax.ShapeDtypeStruct((m, width), BF16),
        scratch_shapes=[pltpu.VMEM((dh, dh), F32)],
        compiler_params=_params("parallel", "parallel", "arbitrary"),
        name="retention",
    )(qk, qk, v, sg, decay, xi, zeta, cd)


def _sgu_kernel(u_ref, vg_ref, w_ref, b_ref, o_ref, *, n_chunks):
    row = lax.broadcasted_iota(jnp.int32, (CHUNK, CHUNK), 0)
    col = lax.broadcasted_iota(jnp.int32, (CHUNK, CHUNK), 1)
    w = (w_ref[0] * (row >= col).astype(F32)).astype(BF16)
    bias = b_ref[0]
    for c in range(n_chunks):
        rows = pl.ds(c * CHUNK, CHUNK)
        mixed = jnp.dot(w, vg_ref[rows, :], preferred_element_type=F32) + bias
        o_ref[rows, :] = (u_ref[rows, :].astype(F32) * mixed).astype(o_ref.dtype)


def _sgu(u, vg, w_s, bias, n_chunks=8):
    m, width = u.shape
    dg = width // SGU_GROUPS
    rows = n_chunks * CHUNK
    tok = pl.BlockSpec((rows, dg), lambda i, g: (i, g))
    return pl.pallas_call(
        functools.partial(_sgu_kernel, n_chunks=n_chunks),
        grid=(m // rows, SGU_GROUPS),
        in_specs=[tok, tok,
                  pl.BlockSpec((1, CHUNK, CHUNK), lambda i, g: (g, 0, 0)),
                  pl.BlockSpec((1, CHUNK, dg), lambda i, g: (g, 0, 0))],
        out_specs=tok,
        out_shape=jax.ShapeDtypeStruct((m, width), BF16),
        compiler_params=_params("parallel", "parallel"),
        name="sgu",
    )(u, vg, w_s, bias)


def _retention_tables(dh):
    log_gamma = jnp.log(1.0 - 2.0 ** (-5.0 - jnp.arange(RET_HEADS, dtype=F32)))
    idx = jnp.arange(CHUNK)
    dist = (idx[:, None] - idx[None, :]).astype(F32)
    decay = jnp.where(dist[None] >= 0, jnp.exp(log_gamma[:, None, None] * jnp.maximum(dist, 0.0)[None]), 0.0)
    xi = jnp.exp(log_gamma[:, None] * (idx + 1).astype(F32))
    zeta = jnp.exp(log_gamma[:, None] * (CHUNK - 1 - idx).astype(F32))
    cd = jnp.exp(log_gamma * CHUNK)
    bcast = lambda t: jnp.broadcast_to(t[:, :, None], (RET_HEADS, CHUNK, dh))
    return decay, bcast(xi), bcast(zeta), jnp.broadcast_to(cd[:, None, None], (RET_HEADS, 1, dh))


def kernel(x, c, positions, ada_w, ada_b, norm_ffn1_g, ffn1_w1, ffn1_w3, ffn1_w2, norm_mix_g, w_in, sgu_norm_g, sgu_norm_b, sgu_w_s, sgu_b_s, w_out, norm_ffn2_g, ffn2_w1, ffn2_w3, ffn2_w2, final_norm_g):
    batch, seq, d = x.shape
    m = batch * seq
    depth = ada_w.shape[0]
    ret_width = w_out.shape[1] // 2
    sgu_width = w_out.shape[1] - ret_width
    dh = ret_width // RET_HEADS
    dg = sgu_width // SGU_GROUPS

    x2 = x.reshape(m, d)
    c_pad = jnp.zeros((8, d), F32).at[:batch].set(c)
    inv_freq = ROPE_BASE ** (-jnp.arange(0, dh // 2, dtype=F32) / (dh // 2))
    cos, sin = _rope_tables(positions.reshape(m, 1), inv_freq)
    decay, xi, zeta, cd = _retention_tables(dh)
    rope_specs = [pl.BlockSpec((1024, dh // 2), lambda i, j: (i, 0))] * 2

    for l in range(depth):
        mod3 = _ada_mod(c_pad, ada_w[l], ada_b[l])[:batch].reshape(batch, 1, N_MOD * d)

        h = _norm_mod(x2, norm_ffn1_g[l], mod3, 0, 1, seq)
        act = _ffn_up(h, ffn1_w1[l].astype(BF16), ffn1_w3[l].astype(BF16))
        x2 = _matmul_residual([act], [(ffn1_w2[l].astype(BF16), 0, act.shape[1])], x2, mod3, 2, 0.5, seq,
                              tm=512, tn=512)

        h = _norm_mod(x2, norm_mix_g[l], mod3, 3, 4, seq)
        w_in_b = w_in[l].astype(BF16)
        qk = _proj(functools.partial(_proj_rotary_kernel, head_dim=dh, k_blocks_from=ret_width // 1024,
                                     k_scale=dh ** -0.5),
                   h, w_in_b, 0, 2 * ret_width, [cos, sin], rope_specs)
        v = _proj(functools.partial(_proj_act_kernel, act=lambda t: t), h, w_in_b, 2 * ret_width, ret_width, [], [])
        sg = _proj(functools.partial(_proj_act_kernel, act=_silu), h, w_in_b, 3 * ret_width, ret_width, [], [])
        u = _proj(functools.partial(_proj_act_kernel, act=_gelu_tanh), h, w_in_b, 4 * ret_width, sgu_width, [], [])
        gn_specs = [pl.BlockSpec((1, 1024), lambda i, j: (0, j))] * 2
        vg = _proj(functools.partial(_proj_gelu_groupnorm_kernel, group_dim=dg), h, w_in_b,
                   4 * ret_width + sgu_width, sgu_width,
                   [sgu_norm_g[l].reshape(1, sgu_width), sgu_norm_b[l].reshape(1, sgu_width)], gn_specs)
        ret = _retention(qk, v, sg, decay, xi, zeta, cd, batch, seq)
        bias = jnp.broadcast_to(sgu_b_s[l][:, :, None], (SGU_GROUPS, CHUNK, dg))
        gated = _sgu(u, vg, sgu_w_s[l], bias)
        w_out_b = w_out[l].astype(BF16)
        x2 = _matmul_residual([ret, gated], [(w_out_b, 0, ret_width), (w_out_b, 1, sgu_width)], x2, mod3, 5, 1.0,
                              seq, tm=1024, tn=1024)

        h = _norm_mod(x2, norm_ffn2_g[l], mod3, 6, 7, seq)
        act = _ffn_up(h, ffn2_w1[l].astype(BF16), ffn2_w3[l].astype(BF16))
        x2 = _matmul_residual([act], [(ffn2_w2[l].astype(BF16), 0, act.shape[1])], x2, mod3, 8, 0.5, seq,
                              tm=512, tn=512)

    return _final_norm(x2, final_norm_g).reshape(batch, seq, d)
```

```python
import functools

import jax
import jax.numpy as jnp
from jax import lax
from jax.experimental import pallas as pl
from jax.experimental.pallas import tpu as pltpu

F32 = jnp.float32
BF16 = jnp.bfloat16

RET_HEADS = 8
SGU_GROUPS = 8
CHUNK = 128
ROPE_BASE = 10000.0
EPS = 1e-6
N_MOD = 9

V7X_VMEM_BYTES = 64 * 1024 * 1024
VMEM_LIMIT_BYTES = V7X_VMEM_BYTES - 8 * 1024 * 1024


def _params(*semantics):
    return pltpu.CompilerParams(dimension_semantics=semantics, vmem_limit_bytes=VMEM_LIMIT_BYTES)


def _silu(x):
    return x / (1.0 + jnp.exp(-x))


def _gelu_tanh(x):
    return 0.5 * x * (1.0 + jnp.tanh(0.7978845608028654 * (x + 0.044715 * (x * x * x))))


def _ada_kernel(c_ref, w_ref, b_ref, o_ref):
    sc = _silu(c_ref[...]).astype(BF16)
    o_ref[...] = jnp.dot(sc, w_ref[...].astype(BF16), preferred_element_type=F32) + b_ref[...]


def _ada_mod(c_pad, w, b, tn=512):
    rows, d = c_pad.shape
    n = w.shape[1]
    return pl.pallas_call(
        _ada_kernel,
        grid=(n // tn,),
        in_specs=[
            pl.BlockSpec((rows, d), lambda j: (0, 0)),
            pl.BlockSpec((d, tn), lambda j: (0, j)),
            pl.BlockSpec((1, tn), lambda j: (0, j)),
        ],
        out_specs=pl.BlockSpec((rows, tn), lambda j: (0, j)),
        out_shape=jax.ShapeDtypeStruct((rows, n), F32),
        compiler_params=_params("parallel"),
        name="ada_mod",
    )(c_pad, w, b.reshape(1, n))


def _rms(x, g):
    return x * lax.rsqrt(jnp.mean(x * x, axis=-1, keepdims=True) + EPS) * g


def _norm_mod_kernel(x_ref, g_ref, sh_ref, sc_ref, o_ref):
    y = _rms(x_ref[...], g_ref[...])
    o_ref[...] = (y * (1.0 + sc_ref[0]) + sh_ref[0]).astype(o_ref.dtype)


def _norm_kernel(x_ref, g_ref, o_ref):
    o_ref[...] = _rms(x_ref[...], g_ref[...]).astype(o_ref.dtype)


def _norm_mod(x2, g, mod3, seg_shift, seg_scale, seq, tm=256):
    m, d = x2.shape
    bidx = lambda i: (i * tm) // seq
    return pl.pallas_call(
        _norm_mod_kernel,
        grid=(m // tm,),
        in_specs=[
            pl.BlockSpec((tm, d), lambda i: (i, 0)),
            pl.BlockSpec((1, d), lambda i: (0, 0)),
            pl.BlockSpec((1, 1, d), lambda i: (bidx(i), 0, seg_shift)),
            pl.BlockSpec((1, 1, d), lambda i: (bidx(i), 0, seg_scale)),
        ],
        out_specs=pl.BlockSpec((tm, d), lambda i: (i, 0)),
        out_shape=jax.ShapeDtypeStruct((m, d), BF16),
        compiler_params=_params("parallel"),
        name="norm_mod",
    )(x2, g.reshape(1, d), mod3, mod3)


def _final_norm(x2, g, tm=256):
    m, d = x2.shape
    return pl.pallas_call(
        _norm_kernel,
        grid=(m // tm,),
        in_specs=[pl.BlockSpec((tm, d), lambda i: (i, 0)), pl.BlockSpec((1, d), lambda i: (0, 0))],
        out_specs=pl.BlockSpec((tm, d), lambda i: (i, 0)),
        out_shape=jax.ShapeDtypeStruct((m, d), F32),
        compiler_params=_params("parallel"),
        name="final_norm",
    )(x2, g.reshape(1, d))


def _rope_kernel(pos_ref, freq_ref, cos_ref, sin_ref):
    ang = pos_ref[...].astype(F32) * freq_ref[...]
    cos_ref[...] = jnp.cos(ang)
    sin_ref[...] = jnp.sin(ang)


def _rope_tables(pos_col, inv_freq, tm=1024):
    m = pos_col.shape[0]
    half = inv_freq.shape[0]
    spec = pl.BlockSpec((tm, half), lambda i: (i, 0))
    return pl.pallas_call(
        _rope_kernel,
        grid=(m // tm,),
        in_specs=[pl.BlockSpec((tm, 1), lambda i: (i, 0)), pl.BlockSpec((1, half), lambda i: (0, 0))],
        out_specs=[spec, spec],
        out_shape=[jax.ShapeDtypeStruct((m, half), F32)] * 2,
        compiler_params=_params("parallel"),
        name="rope_tables",
    )(pos_col, inv_freq.reshape(1, half))


def _ffn_up_kernel(*refs, n_side):
    h_ref, w1_ref, w3_ref = refs[:3]
    side_in = refs[3:3 + n_side]
    o_ref = refs[3 + n_side]
    side_out = refs[4 + n_side:]
    h = h_ref[...]
    a = jnp.dot(h, w1_ref[...].astype(BF16), preferred_element_type=F32)
    b = jnp.dot(h, w3_ref[...].astype(BF16), preferred_element_type=F32)
    o_ref[...] = (_silu(a) * b).astype(o_ref.dtype)
    for src, dst in zip(side_in, side_out):
        dst[...] = src[...].astype(dst.dtype)


def _ffn_up(h, w1, w3, side, tm=2048, tn=256):
    m, k = h.shape
    f = w1.shape[1]
    steps_j = f // tn
    wspec = pl.BlockSpec((k, tn), lambda i, j: (0, j))
    side_specs, side_shapes = [], []
    for w, rb in side:
        nb = w.shape[0] // rb
        assert nb * rb == w.shape[0] and nb <= (m // tm) * steps_j
        side_specs.append(pl.BlockSpec(
            (rb, w.shape[1]), lambda i, j, nb=nb: (jnp.minimum(i * steps_j + j, nb - 1), 0)))
        side_shapes.append(jax.ShapeDtypeStruct(w.shape, BF16))
    outs = pl.pallas_call(
        functools.partial(_ffn_up_kernel, n_side=len(side)),
        grid=(m // tm, steps_j),
        in_specs=[pl.BlockSpec((tm, k), lambda i, j: (i, 0), pipeline_mode=pl.Buffered(1)), wspec, wspec]
        + side_specs,
        out_specs=[pl.BlockSpec((tm, tn), lambda i, j: (i, j))] + side_specs,
        out_shape=[jax.ShapeDtypeStruct((m, f), BF16)] + side_shapes,
        compiler_params=_params("arbitrary", "arbitrary"),
        name="ffn_up",
    )(h, w1, w3, *[w for w, _ in side])
    return outs[0], outs[1:]


def _residual_kernel(*refs, n_pairs, scale):
    a_refs = refs[:n_pairs]
    b_refs = refs[n_pairs:2 * n_pairs]
    x_ref, gate_ref, o_ref = refs[2 * n_pairs:]
    acc = jnp.dot(a_refs[0][...], b_refs[0][...], preferred_element_type=F32)
    for a_ref, b_ref in zip(a_refs[1:], b_refs[1:]):
        acc = acc + jnp.dot(a_ref[...], b_ref[...], preferred_element_type=F32)
    o_ref[...] = x_ref[...] + (scale * gate_ref[0]) * acc


def _matmul_residual(a_list, b_list, x2, mod3, seg_gate, scale, seq, tm, tn):
    m, n = x2.shape
    n_pairs = len(a_list)
    in_specs = []
    for a in a_list:
        in_specs.append(pl.BlockSpec((tm, a.shape[1]), lambda j, i: (i, 0)))
    operands = list(a_list)
    for w, row_block, rows in b_list:
        in_specs.append(pl.BlockSpec((rows, tn), lambda j, i, rb=row_block: (rb, j)))
        operands.append(w)
    in_specs.append(pl.BlockSpec((tm, tn), lambda j, i: (i, j)))
    in_specs.append(pl.BlockSpec((1, 1, tn), lambda j, i: ((i * tm) // seq, 0, seg_gate * (n // tn) + j)))
    return pl.pallas_call(
        functools.partial(_residual_kernel, n_pairs=n_pairs, scale=scale),
        grid=(n // tn, m // tm),
        in_specs=in_specs,
        out_specs=pl.BlockSpec((tm, tn), lambda j, i: (i, j)),
        out_shape=jax.ShapeDtypeStruct((m, n), F32),
        compiler_params=_params("parallel", "parallel"),
        name="matmul_residual",
    )(*operands, x2, mod3)


def _proj_rotary_kernel(h_ref, w_ref, cos_ref, sin_ref, o_ref, *, head_dim, k_blocks_from, k_scale):
    acc = jnp.dot(h_ref[...], w_ref[...], preferred_element_type=F32)
    scale = jnp.where(pl.program_id(1) >= k_blocks_from, k_scale, 1.0).astype(F32)
    cos = cos_ref[...] * scale
    sin = sin_ref[...] * scale
    half = head_dim // 2
    for hd in range(acc.shape[1] // head_dim):
        t1 = acc[:, hd * head_dim:hd * head_dim + half]
        t2 = acc[:, hd * head_dim + half:(hd + 1) * head_dim]
        o_ref[:, hd * head_dim:hd * head_dim + half] = (t1 * cos - t2 * sin).astype(o_ref.dtype)
        o_ref[:, hd * head_dim + half:(hd + 1) * head_dim] = (t1 * sin + t2 * cos).astype(o_ref.dtype)


def _proj_act_kernel(h_ref, w_ref, o_ref, *, act):
    acc = jnp.dot(h_ref[...], w_ref[...], preferred_element_type=F32)
    o_ref[...] = act(acc).astype(o_ref.dtype)


def _proj_gelu_groupnorm_kernel(h_ref, w_ref, g_ref, b_ref, o_ref, *, group_dim):
    acc = jnp.dot(h_ref[...], w_ref[...], preferred_element_type=F32)
    for gi in range(acc.shape[1] // group_dim):
        cols = slice(gi * group_dim, (gi + 1) * group_dim)
        v = _gelu_tanh(acc[:, cols])
        mu = jnp.mean(v, axis=-1, keepdims=True)
        cen = v - mu
        var = jnp.mean(cen * cen, axis=-1, keepdims=True)
        o_ref[:, cols] = (cen * lax.rsqrt(var + EPS) * g_ref[:, cols] + b_ref[:, cols]).astype(o_ref.dtype)


def _proj(kernel, h, w, col0, n, extra, extra_specs, tm=1024, tn=1024):
    m, k = h.shape
    cb0 = col0 // tn
    return pl.pallas_call(
        kernel,
        grid=(m // tm, n // tn),
        in_specs=[pl.BlockSpec((tm, k), lambda i, j: (i, 0)),
                  pl.BlockSpec((k, tn), lambda i, j: (0, cb0 + j))] + extra_specs,
        out_specs=pl.BlockSpec((tm, tn), lambda i, j: (i, j)),
        out_shape=jax.ShapeDtypeStruct((m, n), BF16),
        compiler_params=_params("parallel", "parallel"),
        name="proj",
    )(h, w, *extra)


def _retention_kernel(q_ref, k_ref, v_ref, g_ref, decay_ref, xi_ref, zeta_ref, cd_ref, o_ref, state_ref,
                      *, n_chunks):
    @pl.when(pl.program_id(2) == 0)
    def _():
        state_ref[...] = jnp.zeros_like(state_ref)

    decay = decay_ref[0]
    xi = xi_ref[0]
    zeta = zeta_ref[0]
    cd = cd_ref[0]
    for c in range(n_chunks):
        rows = pl.ds(c * CHUNK, CHUNK)
        q = q_ref[rows, :]
        k = k_ref[rows, :]
        v = v_ref[rows, :]
        state = state_ref[...]
        scores = lax.dot_general(q, k, (((1,), (1,)), ((), ())), preferred_element_type=F32) * decay
        intra = jnp.dot(scores.astype(BF16), v, preferred_element_type=F32)
        inter = jnp.dot(q, state.astype(BF16), preferred_element_type=F32) * xi
        kz = (k.astype(F32) * zeta).astype(BF16)
        kv = lax.dot_general(kz, v, (((0,), (0,)), ((), ())), preferred_element_type=F32)
        state_ref[...] = state * cd + kv
        out = intra + inter
        out = out * lax.rsqrt(jnp.mean(out * out, axis=-1, keepdims=True) + EPS)
        o_ref[rows, :] = (out * g_ref[rows, :].astype(F32)).astype(o_ref.dtype)


def _retention(qk, v, sg, decay, xi, zeta, cd, batch, seq, n_chunks=16):
    m, width = v.shape
    dh = width // RET_HEADS
    rows = n_chunks * CHUNK
    steps = seq // rows
    row_block = lambda b, h, c: b * steps + c
    tok = lambda col0: pl.BlockSpec((rows, dh), lambda b, h, c: (row_block(b, h, c), col0 + h))
    return pl.pallas_call(
        functools.partial(_retention_kernel, n_chunks=n_chunks),
        grid=(batch, RET_HEADS, steps),
        in_specs=[
            tok(0), tok(RET_HEADS), tok(0), tok(0),
            pl.BlockSpec((1, CHUNK, CHUNK), lambda b, h, c: (h, 0, 0)),
            pl.BlockSpec((1, CHUNK, dh), lambda b, h, c: (h, 0, 0)),
            pl.BlockSpec((1, CHUNK, dh), lambda b, h, c: (h, 0, 0)),
            pl.BlockSpec((1, 1, dh), lambda b, h, c: (h, 0, 0)),
        ],
        out_specs=tok(0),
        out_shape=jax.ShapeDtypeStruct((m, width), BF16),
        scratch_shapes=[pltpu.VMEM((dh, dh), F32)],
        compiler_params=_params("parallel", "parallel", "arbitrary"),
        name="retention",
    )(qk, qk, v, sg, decay, xi, zeta, cd)


def _sgu_kernel(u_ref, vg_ref, w_ref, b_ref, o_ref, *, n_chunks):
    row = lax.broadcasted_iota(jnp.int32, (CHUNK, CHUNK), 0)
    col = lax.broadcasted_iota(jnp.int32, (CHUNK, CHUNK), 1)
    w = (w_ref[0] * (row >= col).astype(F32)).astype(BF16)
    bias = b_ref[0]
    for c in range(n_chunks):
        rows = pl.ds(c * CHUNK, CHUNK)
        mixed = jnp.dot(w, vg_ref[rows, :], preferred_element_type=F32) + bias
        o_ref[rows, :] = (u_ref[rows, :].astype(F32) * mixed).astype(o_ref.dtype)


def _sgu(u, vg, w_s, bias, n_chunks=8):
    m, width = u.shape
    dg = width // SGU_GROUPS
    rows = n_chunks * CHUNK
    tok = pl.BlockSpec((rows, dg), lambda i, g: (i, g))
    return pl.pallas_call(
        functools.partial(_sgu_kernel, n_chunks=n_chunks),
        grid=(m // rows, SGU_GROUPS),
        in_specs=[tok, tok,
                  pl.BlockSpec((1, CHUNK, CHUNK), lambda i, g: (g, 0, 0)),
                  pl.BlockSpec((1, CHUNK, dg), lambda i, g: (g, 0, 0))],
        out_specs=tok,
        out_shape=jax.ShapeDtypeStruct((m, width), BF16),
        compiler_params=_params("parallel", "parallel"),
        name="sgu",
    )(u, vg, w_s, bias)


def _retention_tables(dh):
    log_gamma = jnp.log(1.0 - 2.0 ** (-5.0 - jnp.arange(RET_HEADS, dtype=F32)))
    idx = jnp.arange(CHUNK)
    dist = (idx[:, None] - idx[None, :]).astype(F32)
    decay = jnp.where(dist[None] >= 0, jnp.exp(log_gamma[:, None, None] * jnp.maximum(dist, 0.0)[None]), 0.0)
    xi = jnp.exp(log_gamma[:, None] * (idx + 1).astype(F32))
    zeta = jnp.exp(log_gamma[:, None] * (CHUNK - 1 - idx).astype(F32))
    cd = jnp.exp(log_gamma * CHUNK)
    bcast = lambda t: jnp.broadcast_to(t[:, :, None], (RET_HEADS, CHUNK, dh))
    return decay, bcast(xi), bcast(zeta), jnp.broadcast_to(cd[:, None, None], (RET_HEADS, 1, dh))


def kernel(x, c, positions, ada_w, ada_b, norm_ffn1_g, ffn1_w1, ffn1_w3, ffn1_w2, norm_mix_g, w_in, sgu_norm_g, sgu_norm_b, sgu_w_s, sgu_b_s, w_out, norm_ffn2_g, ffn2_w1, ffn2_w3, ffn2_w2, final_norm_g):
    batch, seq, d = x.shape
    m = batch * seq
    depth = ada_w.shape[0]
    ret_width = w_out.shape[1] // 2
    sgu_width = w_out.shape[1] - ret_width
    dh = ret_width // RET_HEADS
    dg = sgu_width // SGU_GROUPS

    x2 = x.reshape(m, d)
    c_pad = jnp.zeros((8, d), F32).at[:batch].set(c)
    inv_freq = ROPE_BASE ** (-jnp.arange(0, dh // 2, dtype=F32) / (dh // 2))
    cos, sin = _rope_tables(positions.reshape(m, 1), inv_freq)
    decay, xi, zeta, cd = _retention_tables(dh)
    rope_specs = [pl.BlockSpec((1024, dh // 2), lambda i, j: (i, 0))] * 2

    for l in range(depth):
        mod3 = _ada_mod(c_pad, ada_w[l], ada_b[l])[:batch].reshape(batch, 1, N_MOD * d)

        h = _norm_mod(x2, norm_ffn1_g[l], mod3, 0, 1, seq)
        act, (w2_b, w_in_b, w_out_b) = _ffn_up(h, ffn1_w1[l], ffn1_w3[l],
                                               [(ffn1_w2[l], 32), (w_in[l], 16), (w_out[l], 16)])
        x2 = _matmul_residual([act], [(w2_b, 0, act.shape[1])], x2, mod3, 2, 0.5, seq, tm=512, tn=512)

        h = _norm_mod(x2, norm_mix_g[l], mod3, 3, 4, seq)
        qk = _proj(functools.partial(_proj_rotary_kernel, head_dim=dh, k_blocks_from=ret_width // 1024,
                                     k_scale=dh ** -0.5),
                   h, w_in_b, 0, 2 * ret_width, [cos, sin], rope_specs)
        v = _proj(functools.partial(_proj_act_kernel, act=lambda t: t), h, w_in_b, 2 * ret_width, ret_width, [], [])
        sg = _proj(functools.partial(_proj_act_kernel, act=_silu), h, w_in_b, 3 * ret_width, ret_width, [], [])
        u = _proj(functools.partial(_proj_act_kernel, act=_gelu_tanh), h, w_in_b, 4 * ret_width, sgu_width, [], [])
        gn_specs = [pl.BlockSpec((1, 1024), lambda i, j: (0, j))] * 2
        vg = _proj(functools.partial(_proj_gelu_groupnorm_kernel, group_dim=dg), h, w_in_b,
                   4 * ret_width + sgu_width, sgu_width,
                   [sgu_norm_g[l].reshape(1, sgu_width), sgu_norm_b[l].reshape(1, sgu_width)], gn_specs)
        ret = _retention(qk, v, sg, decay, xi, zeta, cd, batch, seq)
        bias = jnp.broadcast_to(sgu_b_s[l][:, :, None], (SGU_GROUPS, CHUNK, dg))
        gated = _sgu(u, vg, sgu_w_s[l], bias)
        x2 = _matmul_residual([ret, gated], [(w_out_b, 0, ret_width), (w_out_b, 1, sgu_width)], x2, mod3, 5, 1.0,
                              seq, tm=1024, tn=1024)

        h = _norm_mod(x2, norm_ffn2_g[l], mod3, 6, 7, seq)
        act, (w2_b,) = _ffn_up(h, ffn2_w1[l], ffn2_w3[l], [(ffn2_w2[l], 32)])
        x2 = _matmul_residual([act], [(w2_b, 0, act.shape[1])], x2, mod3, 8, 0.5, seq, tm=512, tn=512)

    return _final_norm(x2, final_norm_g).reshape(batch, seq, d)
```

```python
import functools

import jax
import jax.numpy as jnp
from jax import lax
from jax.experimental import pallas as pl
from jax.experimental.pallas import tpu as pltpu

F32 = jnp.float32
BF16 = jnp.bfloat16

RET_HEADS = 8
SGU_GROUPS = 8
CHUNK = 128
ROPE_BASE = 10000.0
EPS = 1e-6
N_MOD = 9

V7X_VMEM_BYTES = 64 * 1024 * 1024
VMEM_LIMIT_BYTES = V7X_VMEM_BYTES - 8 * 1024 * 1024


def _params(*semantics):
    return pltpu.CompilerParams(dimension_semantics=semantics, vmem_limit_bytes=VMEM_LIMIT_BYTES)


def _silu(x):
    return x / (1.0 + jnp.exp(-x))


def _gelu_tanh(x):
    return 0.5 * x * (1.0 + jnp.tanh(0.7978845608028654 * (x + 0.044715 * (x * x * x))))


def _ada_kernel(c_ref, w_ref, b_ref, o_ref):
    sc = _silu(c_ref[...]).astype(BF16)
    o_ref[...] = jnp.dot(sc, w_ref[...].astype(BF16), preferred_element_type=F32) + b_ref[...]


def _ada_mod(c_pad, w, b, tn=512):
    rows, d = c_pad.shape
    n = w.shape[1]
    return pl.pallas_call(
        _ada_kernel,
        grid=(n // tn,),
        in_specs=[
            pl.BlockSpec((rows, d), lambda j: (0, 0)),
            pl.BlockSpec((d, tn), lambda j: (0, j)),
            pl.BlockSpec((1, tn), lambda j: (0, j)),
        ],
        out_specs=pl.BlockSpec((rows, tn), lambda j: (0, j)),
        out_shape=jax.ShapeDtypeStruct((rows, n), F32),
        compiler_params=_params("parallel"),
        name="ada_mod",
    )(c_pad, w, b.reshape(1, n))


def _rms(x, g):
    return x * lax.rsqrt(jnp.mean(x * x, axis=-1, keepdims=True) + EPS) * g


def _norm_mod_kernel(x_ref, g_ref, sh_ref, sc_ref, o_ref):
    y = _rms(x_ref[...], g_ref[...])
    o_ref[...] = (y * (1.0 + sc_ref[0]) + sh_ref[0]).astype(o_ref.dtype)


def _norm_kernel(x_ref, g_ref, o_ref):
    o_ref[...] = _rms(x_ref[...], g_ref[...]).astype(o_ref.dtype)


def _norm_mod(x2, g, mod3, seg_shift, seg_scale, seq, tm=512):
    m, d = x2.shape
    bidx = lambda i: (i * tm) // seq
    return pl.pallas_call(
        _norm_mod_kernel,
        grid=(m // tm,),
        in_specs=[
            pl.BlockSpec((tm, d), lambda i: (i, 0)),
            pl.BlockSpec((1, d), lambda i: (0, 0)),
            pl.BlockSpec((1, 1, d), lambda i: (bidx(i), 0, seg_shift)),
            pl.BlockSpec((1, 1, d), lambda i: (bidx(i), 0, seg_scale)),
        ],
        out_specs=pl.BlockSpec((tm, d), lambda i: (i, 0)),
        out_shape=jax.ShapeDtypeStruct((m, d), BF16),
        compiler_params=_params("parallel"),
        name="norm_mod",
    )(x2, g.reshape(1, d), mod3, mod3)


def _final_norm(x2, g, tm=512):
    m, d = x2.shape
    return pl.pallas_call(
        _norm_kernel,
        grid=(m // tm,),
        in_specs=[pl.BlockSpec((tm, d), lambda i: (i, 0)), pl.BlockSpec((1, d), lambda i: (0, 0))],
        out_specs=pl.BlockSpec((tm, d), lambda i: (i, 0)),
        out_shape=jax.ShapeDtypeStruct((m, d), F32),
        compiler_params=_params("parallel"),
        name="final_norm",
    )(x2, g.reshape(1, d))


def _rope_kernel(pos_ref, freq_ref, cos_ref, sin_ref):
    ang = pos_ref[...].astype(F32) * freq_ref[...]
    cos_ref[...] = jnp.cos(ang)
    sin_ref[...] = jnp.sin(ang)


def _rope_tables(pos_col, inv_freq, tm=1024):
    m = pos_col.shape[0]
    half = inv_freq.shape[0]
    spec = pl.BlockSpec((tm, half), lambda i: (i, 0))
    return pl.pallas_call(
        _rope_kernel,
        grid=(m // tm,),
        in_specs=[pl.BlockSpec((tm, 1), lambda i: (i, 0)), pl.BlockSpec((1, half), lambda i: (0, 0))],
        out_specs=[spec, spec],
        out_shape=[jax.ShapeDtypeStruct((m, half), F32)] * 2,
        compiler_params=_params("parallel"),
        name="rope_tables",
    )(pos_col, inv_freq.reshape(1, half))


def _as_bf16(ref):
    w = ref[...]
    return w if w.dtype == BF16 else w.astype(BF16)


def _ffn_up_kernel(*refs, n_tiles, n_side, last_tile):
    h_ref = refs[0]
    w1_refs = refs[1:1 + n_tiles]
    w3_refs = refs[1 + n_tiles:1 + 2 * n_tiles]
    side_in = refs[1 + 2 * n_tiles:1 + 2 * n_tiles + n_side]
    o_ref = refs[1 + 2 * n_tiles + n_side]
    side_out = refs[2 + 2 * n_tiles + n_side:]
    tn = w1_refs[0].shape[1]

    def tile(t):
        h = h_ref[...]
        a = jnp.dot(h, _as_bf16(w1_refs[t]), preferred_element_type=F32)
        b = jnp.dot(h, _as_bf16(w3_refs[t]), preferred_element_type=F32)
        o_ref[:, t * tn:(t + 1) * tn] = (_silu(a) * b).astype(o_ref.dtype)

    if n_tiles == 1:
        tile(0)
    else:
        first = pl.program_id(1) * n_tiles

        @pl.when(first + n_tiles - 1 <= last_tile)
        def _():
            for t in range(n_tiles):
                tile(t)

        @pl.when(first + n_tiles - 1 > last_tile)
        def _():
            for t in range(n_tiles):
                if_valid = first + t <= last_tile
                pl.when(if_valid)(functools.partial(tile, t))

                @pl.when(jnp.logical_not(if_valid))
                def _():
                    o_ref[:, t * tn:(t + 1) * tn] = jnp.zeros((o_ref.shape[0], tn), o_ref.dtype)

    for src, dst in zip(side_in, side_out):
        dst[...] = src[...].astype(dst.dtype)


def _ffn_up(h, w1, w3, side, n_tiles, tm=2048, tn=256):
    m, k = h.shape
    f = w1.shape[1]
    last_tile = f // tn - 1
    steps_j = pl.cdiv(last_tile + 1, n_tiles)
    wspecs = [pl.BlockSpec((k, tn), lambda i, j, t=t: (0, jnp.minimum(j * n_tiles + t, last_tile)))
              for t in range(n_tiles)]
    side_specs, side_shapes = [], []
    for w, rb in side:
        nb = w.shape[0] // rb
        assert nb * rb == w.shape[0] and nb <= (m // tm) * steps_j
        side_specs.append(pl.BlockSpec(
            (rb, w.shape[1]), lambda i, j, nb=nb: (jnp.minimum(i * steps_j + j, nb - 1), 0)))
        side_shapes.append(jax.ShapeDtypeStruct(w.shape, BF16))
    outs = pl.pallas_call(
        functools.partial(_ffn_up_kernel, n_tiles=n_tiles, n_side=len(side), last_tile=last_tile),
        grid=(m // tm, steps_j),
        in_specs=[pl.BlockSpec((tm, k), lambda i, j: (i, 0), pipeline_mode=pl.Buffered(1))]
        + wspecs + wspecs + side_specs,
        out_specs=[pl.BlockSpec((tm, n_tiles * tn), lambda i, j: (i, j))] + side_specs,
        out_shape=[jax.ShapeDtypeStruct((m, steps_j * n_tiles * tn), BF16)] + side_shapes,
        compiler_params=_params("arbitrary", "arbitrary"),
        name="ffn_up",
    )(h, *([w1] * n_tiles), *([w3] * n_tiles), *[w for w, _ in side])
    return outs[0], outs[1:]


def _residual_kernel(*refs, n_pairs, scale):
    a_refs = refs[:n_pairs]
    b_refs = refs[n_pairs:2 * n_pairs]
    x_ref, gate_ref, o_ref = refs[2 * n_pairs:]
    acc = None
    for a_ref, b_ref in zip(a_refs, b_refs):
        part = jnp.dot(a_ref[:, :b_ref.shape[0]], b_ref[...], preferred_element_type=F32)
        acc = part if acc is None else acc + part
    o_ref[...] = x_ref[...] + (scale * gate_ref[0]) * acc


def _matmul_residual(a_list, b_list, x2, mod3, seg_gate, scale, seq, tm, tn):
    m, n = x2.shape
    n_pairs = len(a_list)
    in_specs = []
    for a in a_list:
        in_specs.append(pl.BlockSpec((tm, a.shape[1]), lambda j, i: (i, 0)))
    operands = list(a_list)
    for w, row_block, rows in b_list:
        in_specs.append(pl.BlockSpec((rows, tn), lambda j, i, rb=row_block: (rb, j)))
        operands.append(w)
    in_specs.append(pl.BlockSpec((tm, tn), lambda j, i: (i, j)))
    in_specs.append(pl.BlockSpec((1, 1, tn), lambda j, i: ((i * tm) // seq, 0, seg_gate * (n // tn) + j)))
    return pl.pallas_call(
        functools.partial(_residual_kernel, n_pairs=n_pairs, scale=scale),
        grid=(n // tn, m // tm),
        in_specs=in_specs,
        out_specs=pl.BlockSpec((tm, tn), lambda j, i: (i, j)),
        out_shape=jax.ShapeDtypeStruct((m, n), F32),
        compiler_params=_params("parallel", "parallel"),
        name="matmul_residual",
    )(*operands, x2, mod3)


def _proj_kernel(h_ref, w_ref, cos_ref, sin_ref, gn_g_ref, gn_b_ref, o_ref, *, ends, head_dim, group_dim, k_scale,
                 n_sub):
    j = pl.program_id(1)
    q_end, k_end, v_end, g_end, u_end = ends
    sub = h_ref.shape[0] // n_sub
    tn = w_ref.shape[1]

    def run(epilogue):
        for r in range(n_sub):
            rows = slice(r * sub, (r + 1) * sub)
            acc = jnp.dot(h_ref[rows, :], w_ref[...], preferred_element_type=F32)
            epilogue(acc, rows)

    def rotary(acc, rows):
        scale = jnp.where(j >= q_end, k_scale, 1.0).astype(F32)
        cos = cos_ref[rows, :] * scale
        sin = sin_ref[rows, :] * scale
        half = head_dim // 2
        for hd in range(tn // head_dim):
            lo = slice(hd * head_dim, hd * head_dim + half)
            hi = slice(hd * head_dim + half, (hd + 1) * head_dim)
            t1, t2 = acc[:, lo], acc[:, hi]
            o_ref[rows, lo] = (t1 * cos - t2 * sin).astype(o_ref.dtype)
            o_ref[rows, hi] = (t1 * sin + t2 * cos).astype(o_ref.dtype)

    def pointwise(act):
        def epilogue(acc, rows):
            o_ref[rows, :] = act(acc).astype(o_ref.dtype)
        return epilogue

    def gelu_groupnorm(acc, rows):
        for gi in range(tn // group_dim):
            cols = slice(gi * group_dim, (gi + 1) * group_dim)
            v = _gelu_tanh(acc[:, cols])
            mu = jnp.mean(v, axis=-1, keepdims=True)
            cen = v - mu
            var = jnp.mean(cen * cen, axis=-1, keepdims=True)
            o_ref[rows, cols] = (cen * lax.rsqrt(var + EPS) * gn_g_ref[:, cols] + gn_b_ref[:, cols]).astype(
                o_ref.dtype)

    pl.when(j < k_end)(lambda: run(rotary))
    pl.when((j >= k_end) & (j < v_end))(lambda: run(pointwise(lambda t: t)))
    pl.when((j >= v_end) & (j < g_end))(lambda: run(pointwise(_silu)))
    pl.when((j >= g_end) & (j < u_end))(lambda: run(pointwise(_gelu_tanh)))
    pl.when(j >= u_end)(lambda: run(gelu_groupnorm))


def _proj(h, w, cos, sin, gn_g, gn_b, ret_width, sgu_width, head_dim, group_dim, tm=1024, tn=1024, n_sub=2):
    m, k = h.shape
    n = w.shape[1]
    rb, sb = ret_width // tn, sgu_width // tn
    ends = (rb, 2 * rb, 3 * rb, 4 * rb, 4 * rb + sb)
    gn_spec = pl.BlockSpec((1, tn), lambda i, j: (0, jnp.clip(j - ends[4], 0, sb - 1)))
    rope_spec = pl.BlockSpec((tm, head_dim // 2), lambda i, j: (i, 0))
    return pl.pallas_call(
        functools.partial(_proj_kernel, ends=ends, head_dim=head_dim, group_dim=group_dim,
                          k_scale=head_dim ** -0.5, n_sub=n_sub),
        grid=(m // tm, n // tn),
        in_specs=[pl.BlockSpec((tm, k), lambda i, j: (i, 0)), pl.BlockSpec((k, tn), lambda i, j: (0, j)),
                  rope_spec, rope_spec, gn_spec, gn_spec],
        out_specs=pl.BlockSpec((tm, tn), lambda i, j: (i, j)),
        out_shape=jax.ShapeDtypeStruct((m, n), BF16),
        compiler_params=_params("parallel", "arbitrary"),
        name="proj",
    )(h, w, cos, sin, gn_g, gn_b)


def _retention_kernel(q_ref, k_ref, v_ref, g_ref, decay_ref, xi_ref, zeta_ref, cd_ref, o_ref, state_ref,
                      *, n_chunks):
    @pl.when(pl.program_id(2) == 0)
    def _():
        state_ref[...] = jnp.zeros_like(state_ref)

    decay = decay_ref[0]
    xi = xi_ref[0]
    zeta = zeta_ref[0]
    cd = cd_ref[0]
    for c in range(n_chunks):
        rows = pl.ds(c * CHUNK, CHUNK)
        q = q_ref[rows, :]
        k = k_ref[rows, :]
        v = v_ref[rows, :]
        state = state_ref[...]
        scores = lax.dot_general(q, k, (((1,), (1,)), ((), ())), preferred_element_type=F32) * decay
        intra = jnp.dot(scores.astype(BF16), v, preferred_element_type=F32)
        inter = jnp.dot(q, state.astype(BF16), preferred_element_type=F32) * xi
        kz = (k.astype(F32) * zeta).astype(BF16)
        kv = lax.dot_general(kz, v, (((0,), (0,)), ((), ())), preferred_element_type=F32)
        state_ref[...] = state * cd + kv
        out = intra + inter
        out = out * lax.rsqrt(jnp.mean(out * out, axis=-1, keepdims=True) + EPS)
        o_ref[rows, :] = (out * g_ref[rows, :].astype(F32)).astype(o_ref.dtype)


def _retention(proj, width, decay, xi, zeta, cd, batch, seq, n_chunks=16):
    m = proj.shape[0]
    dh = width // RET_HEADS
    rows = n_chunks * CHUNK
    steps = seq // rows
    row_block = lambda b, h, c: b * steps + c
    tok = lambda col0: pl.BlockSpec((rows, dh), lambda b, h, c: (row_block(b, h, c), col0 + h))
    return pl.pallas_call(
        functools.partial(_retention_kernel, n_chunks=n_chunks),
        grid=(batch, RET_HEADS, steps),
        in_specs=[
            tok(0), tok(RET_HEADS), tok(2 * RET_HEADS), tok(3 * RET_HEADS),
            pl.BlockSpec((1, CHUNK, CHUNK), lambda b, h, c: (h, 0, 0)),
            pl.BlockSpec((1, CHUNK, dh), lambda b, h, c: (h, 0, 0)),
            pl.BlockSpec((1, CHUNK, dh), lambda b, h, c: (h, 0, 0)),
            pl.BlockSpec((1, 1, dh), lambda b, h, c: (h, 0, 0)),
        ],
        out_specs=tok(0),
        out_shape=jax.ShapeDtypeStruct((m, width), BF16),
        scratch_shapes=[pltpu.VMEM((dh, dh), F32)],
        compiler_params=_params("parallel", "parallel", "arbitrary"),
        name="retention",
    )(proj, proj, proj, proj, decay, xi, zeta, cd)


def _sgu_kernel(u_ref, vg_ref, w_ref, b_ref, o_ref, *, n_chunks):
    row = lax.broadcasted_iota(jnp.int32, (CHUNK, CHUNK), 0)
    col = lax.broadcasted_iota(jnp.int32, (CHUNK, CHUNK), 1)
    w = (w_ref[0] * (row >= col).astype(F32)).astype(BF16)
    bias = b_ref[0]
    for c in range(n_chunks):
        rows = pl.ds(c * CHUNK, CHUNK)
        mixed = jnp.dot(w, vg_ref[rows, :], preferred_element_type=F32) + bias
        o_ref[rows, :] = (u_ref[rows, :].astype(F32) * mixed).astype(o_ref.dtype)


def _sgu(proj, col0, width, w_s, bias, n_chunks=16):
    m = proj.shape[0]
    dg = width // SGU_GROUPS
    rows = n_chunks * CHUNK
    tok_u = pl.BlockSpec((rows, dg), lambda i, g: (i, col0 // dg + g))
    tok_vg = pl.BlockSpec((rows, dg), lambda i, g: (i, col0 // dg + SGU_GROUPS + g))
    return pl.pallas_call(
        functools.partial(_sgu_kernel, n_chunks=n_chunks),
        grid=(m // rows, SGU_GROUPS),
        in_specs=[tok_u, tok_vg,
                  pl.BlockSpec((1, CHUNK, CHUNK), lambda i, g: (g, 0, 0)),
                  pl.BlockSpec((1, CHUNK, dg), lambda i, g: (g, 0, 0))],
        out_specs=pl.BlockSpec((rows, dg), lambda i, g: (i, g)),
        out_shape=jax.ShapeDtypeStruct((m, width), BF16),
        compiler_params=_params("parallel", "parallel"),
        name="sgu",
    )(proj, proj, w_s, bias)


def _retention_tables(dh):
    log_gamma = jnp.log(1.0 - 2.0 ** (-5.0 - jnp.arange(RET_HEADS, dtype=F32)))
    idx = jnp.arange(CHUNK)
    dist = (idx[:, None] - idx[None, :]).astype(F32)
    decay = jnp.where(dist[None] >= 0, jnp.exp(log_gamma[:, None, None] * jnp.maximum(dist, 0.0)[None]), 0.0)
    xi = jnp.exp(log_gamma[:, None] * (idx + 1).astype(F32))
    zeta = jnp.exp(log_gamma[:, None] * (CHUNK - 1 - idx).astype(F32))
    cd = jnp.exp(log_gamma * CHUNK)
    bcast = lambda t: jnp.broadcast_to(t[:, :, None], (RET_HEADS, CHUNK, dh))
    return decay, bcast(xi), bcast(zeta), jnp.broadcast_to(cd[:, None, None], (RET_HEADS, 1, dh))


def kernel(x, c, positions, ada_w, ada_b, norm_ffn1_g, ffn1_w1, ffn1_w3, ffn1_w2, norm_mix_g, w_in, sgu_norm_g, sgu_norm_b, sgu_w_s, sgu_b_s, w_out, norm_ffn2_g, ffn2_w1, ffn2_w3, ffn2_w2, final_norm_g):
    batch, seq, d = x.shape
    m = batch * seq
    depth = ada_w.shape[0]
    d_ff = ffn1_w2.shape[1]
    ret_width = w_out.shape[1] // 2
    sgu_width = w_out.shape[1] - ret_width
    dh = ret_width // RET_HEADS
    dg = sgu_width // SGU_GROUPS

    x2 = x.reshape(m, d)
    c_pad = jnp.zeros((8, d), F32).at[:batch].set(c)
    inv_freq = ROPE_BASE ** (-jnp.arange(0, dh // 2, dtype=F32) / (dh // 2))
    cos, sin = _rope_tables(positions.reshape(m, 1), inv_freq)
    decay, xi, zeta, cd = _retention_tables(dh)

    for l in range(depth):
        mod3 = _ada_mod(c_pad, ada_w[l], ada_b[l])[:batch].reshape(batch, 1, N_MOD * d)

        h = _norm_mod(x2, norm_ffn1_g[l], mod3, 0, 1, seq)
        act, (w2_b, w_in_b, w_out_b, w1_b, w3_b) = _ffn_up(
            h, ffn1_w1[l], ffn1_w3[l],
            [(ffn1_w2[l], 32), (w_in[l], 16), (w_out[l], 16), (ffn2_w1[l], 16), (ffn2_w3[l], 16)], n_tiles=1)
        x2 = _matmul_residual([act], [(w2_b, 0, d_ff)], x2, mod3, 2, 0.5, seq, tm=512, tn=512)

        h = _norm_mod(x2, norm_mix_g[l], mod3, 3, 4, seq)
        proj = _proj(h, w_in_b, cos, sin, sgu_norm_g[l].reshape(1, sgu_width), sgu_norm_b[l].reshape(1, sgu_width),
                     ret_width, sgu_width, dh, dg)
        ret = _retention(proj, ret_width, decay, xi, zeta, cd, batch, seq)
        bias = jnp.broadcast_to(sgu_b_s[l][:, :, None], (SGU_GROUPS, CHUNK, dg))
        gated = _sgu(proj, 4 * ret_width, sgu_width, sgu_w_s[l], bias)
        x2 = _matmul_residual([ret, gated], [(w_out_b, 0, ret_width), (w_out_b, 1, sgu_width)], x2, mod3, 5, 1.0,
                              seq, tm=1024, tn=1024)

        h = _norm_mod(x2, norm_ffn2_g[l], mod3, 6, 7, seq)
        act, (w2_b,) = _ffn_up(h, w1_b, w3_b, [(ffn2_w2[l], 128)], n_tiles=2)
        x2 = _matmul_residual([act], [(w2_b, 0, d_ff)], x2, mod3, 8, 0.5, seq, tm=512, tn=512)

    return _final_norm(x2, final_norm_g).reshape(batch, seq, d)
```

```python
import functools

import jax
import jax.numpy as jnp
from jax import lax
from jax.experimental import pallas as pl
from jax.experimental.pallas import tpu as pltpu

F32 = jnp.float32
BF16 = jnp.bfloat16

RET_HEADS = 8
SGU_GROUPS = 8
CHUNK = 128
RET_CHUNK = 256
NORM_UNROLL = 8
ROPE_BASE = 10000.0
EPS = 1e-6
N_MOD = 9

V7X_VMEM_BYTES = 64 * 1024 * 1024
VMEM_LIMIT_BYTES = V7X_VMEM_BYTES - 8 * 1024 * 1024


def _params(*semantics):
    return pltpu.CompilerParams(dimension_semantics=semantics, vmem_limit_bytes=VMEM_LIMIT_BYTES)


def _silu(x):
    return x / (1.0 + jnp.exp(-x))


def _gelu_tanh(x):
    return 0.5 * x * (1.0 + jnp.tanh(0.7978845608028654 * (x + 0.044715 * (x * x * x))))


def _ada_kernel(c_ref, w_ref, b_ref, o_ref):
    sc = _silu(c_ref[...]).astype(BF16)
    o_ref[...] = jnp.dot(sc, w_ref[...].astype(BF16), preferred_element_type=F32) + b_ref[...]


def _ada_mod(c_pad, w, b, tn=512):
    rows, d = c_pad.shape
    n = w.shape[1]
    return pl.pallas_call(
        _ada_kernel,
        grid=(n // tn,),
        in_specs=[
            pl.BlockSpec((rows, d), lambda j: (0, 0)),
            pl.BlockSpec((d, tn), lambda j: (0, j)),
            pl.BlockSpec((1, tn), lambda j: (0, j)),
        ],
        out_specs=pl.BlockSpec((rows, tn), lambda j: (0, j)),
        out_shape=jax.ShapeDtypeStruct((rows, n), F32),
        compiler_params=_params("parallel"),
        name="ada_mod",
    )(c_pad, w, b.reshape(1, n))


def _norm_rows(x_ref, o_ref, finish, slab):
    def body(s, carry):
        rows = pl.ds(pl.multiple_of(s * slab, slab), slab)
        x = x_ref[rows, :].astype(F32)
        r = lax.rsqrt(jnp.mean(x * x, axis=-1, keepdims=True) + EPS)
        o_ref[rows, :] = finish(x * r).astype(o_ref.dtype)
        return carry

    lax.fori_loop(0, x_ref.shape[0] // slab, body, 0, unroll=NORM_UNROLL)


def _norm_mod_kernel(x_ref, g_ref, sh_ref, sc_ref, o_ref, *, slab):
    gain = g_ref[...] * (1.0 + sc_ref[0])
    shift = sh_ref[0]
    _norm_rows(x_ref, o_ref, lambda y: y * gain + shift, slab)


def _norm_kernel(x_ref, g_ref, o_ref, *, slab):
    g = g_ref[...]
    _norm_rows(x_ref, o_ref, lambda y: y * g, slab)


def _norm_slab(dtype):
    return 8 * (4 // jnp.dtype(dtype).itemsize)


def _norm_mod(x2, g, mod3, seg_shift, seg_scale, seq, tm=512):
    m, d = x2.shape
    bidx = lambda i: (i * tm) // seq
    return pl.pallas_call(
        functools.partial(_norm_mod_kernel, slab=_norm_slab(x2.dtype)),
        grid=(m // tm,),
        in_specs=[
            pl.BlockSpec((tm, d), lambda i: (i, 0)),
            pl.BlockSpec((1, d), lambda i: (0, 0)),
            pl.BlockSpec((1, 1, d), lambda i: (bidx(i), 0, seg_shift)),
            pl.BlockSpec((1, 1, d), lambda i: (bidx(i), 0, seg_scale)),
        ],
        out_specs=pl.BlockSpec((tm, d), lambda i: (i, 0)),
        out_shape=jax.ShapeDtypeStruct((m, d), BF16),
        compiler_params=_params("parallel"),
        name="norm_mod",
    )(x2, g.reshape(1, d), mod3, mod3)


def _final_norm(x2, g, tm=512):
    m, d = x2.shape
    return pl.pallas_call(
        functools.partial(_norm_kernel, slab=_norm_slab(x2.dtype)),
        grid=(m // tm,),
        in_specs=[pl.BlockSpec((tm, d), lambda i: (i, 0)), pl.BlockSpec((1, d), lambda i: (0, 0))],
        out_specs=pl.BlockSpec((tm, d), lambda i: (i, 0)),
        out_shape=jax.ShapeDtypeStruct((m, d), F32),
        compiler_params=_params("parallel"),
        name="final_norm",
    )(x2, g.reshape(1, d))


def _rope_kernel(pos_ref, freq_ref, cos_ref, sin_ref):
    ang = pos_ref[...].astype(F32) * freq_ref[...]
    cos_ref[...] = jnp.cos(ang)
    sin_ref[...] = jnp.sin(ang)


def _rope_tables(pos_col, inv_freq, tm=1024):
    m = pos_col.shape[0]
    half = inv_freq.shape[0]
    spec = pl.BlockSpec((tm, half), lambda i: (i, 0))
    return pl.pallas_call(
        _rope_kernel,
        grid=(m // tm,),
        in_specs=[pl.BlockSpec((tm, 1), lambda i: (i, 0)), pl.BlockSpec((1, half), lambda i: (0, 0))],
        out_specs=[spec, spec],
        out_shape=[jax.ShapeDtypeStruct((m, half), F32)] * 2,
        compiler_params=_params("parallel"),
        name="rope_tables",
    )(pos_col, inv_freq.reshape(1, half))


def _as_bf16(ref):
    w = ref[...]
    return w if w.dtype == BF16 else w.astype(BF16)


def _ffn_up_kernel(*refs, n_tiles, n_side, last_tile):
    h_ref = refs[0]
    w1_refs = refs[1:1 + n_tiles]
    w3_refs = refs[1 + n_tiles:1 + 2 * n_tiles]
    side_in = refs[1 + 2 * n_tiles:1 + 2 * n_tiles + n_side]
    o_ref = refs[1 + 2 * n_tiles + n_side]
    side_out = refs[2 + 2 * n_tiles + n_side:]
    tn = w1_refs[0].shape[1]

    def tile(t):
        h = h_ref[...]
        a = jnp.dot(h, _as_bf16(w1_refs[t]), preferred_element_type=F32)
        b = jnp.dot(h, _as_bf16(w3_refs[t]), preferred_element_type=F32)
        o_ref[:, t * tn:(t + 1) * tn] = (_silu(a) * b).astype(o_ref.dtype)

    if n_tiles == 1:
        tile(0)
    else:
        first = pl.program_id(1) * n_tiles

        @pl.when(first + n_tiles - 1 <= last_tile)
        def _():
            for t in range(n_tiles):
                tile(t)

        @pl.when(first + n_tiles - 1 > last_tile)
        def _():
            for t in range(n_tiles):
                if_valid = first + t <= last_tile
                pl.when(if_valid)(functools.partial(tile, t))

                @pl.when(jnp.logical_not(if_valid))
                def _():
                    o_ref[:, t * tn:(t + 1) * tn] = jnp.zeros((o_ref.shape[0], tn), o_ref.dtype)

    for src, dst in zip(side_in, side_out):
        dst[...] = src[...].astype(dst.dtype)


def _ffn_up(h, w1, w3, side, n_tiles, tm=2048, tn=256):
    m, k = h.shape
    f = w1.shape[1]
    last_tile = f // tn - 1
    steps_j = pl.cdiv(last_tile + 1, n_tiles)
    walk = lambda nb: (lambda i, j: (jnp.minimum(i * steps_j + j, nb - 1), 0))
    wspecs = [pl.BlockSpec((k, tn), lambda i, j, t=t: (0, jnp.minimum(j * n_tiles + t, last_tile)))
              for t in range(n_tiles)]
    side_specs, side_shapes = [], []
    for w, rb in side:
        nb = w.shape[0] // rb
        assert nb * rb == w.shape[0] and nb <= (m // tm) * steps_j
        side_specs.append(pl.BlockSpec((rb, w.shape[1]), walk(nb)))
        side_shapes.append(jax.ShapeDtypeStruct(w.shape, BF16))
    outs = pl.pallas_call(
        functools.partial(_ffn_up_kernel, n_tiles=n_tiles, n_side=len(side), last_tile=last_tile),
        grid=(m // tm, steps_j),
        in_specs=[pl.BlockSpec((tm, k), lambda i, j: (i, 0), pipeline_mode=pl.Buffered(1))]
        + wspecs + wspecs + side_specs,
        out_specs=[pl.BlockSpec((tm, n_tiles * tn), lambda i, j: (i, j))] + side_specs,
        out_shape=[jax.ShapeDtypeStruct((m, steps_j * n_tiles * tn), BF16)] + side_shapes,
        compiler_params=_params("arbitrary", "arbitrary"),
        name="ffn_up",
    )(h, *([w1] * n_tiles), *([w3] * n_tiles), *[w for w, _ in side])
    return outs[0], outs[1:]


def _residual_kernel(*refs, n_pairs, scale):
    a_refs = refs[:n_pairs]
    b_refs = refs[n_pairs:2 * n_pairs]
    x_ref, gate_ref = refs[2 * n_pairs:2 * n_pairs + 2]
    o_refs = refs[2 * n_pairs + 2:]
    acc = None
    for a_ref, b_ref in zip(a_refs, b_refs):
        part = jnp.dot(a_ref[:, :b_ref.shape[0]], b_ref[...], preferred_element_type=F32)
        acc = part if acc is None else acc + part
    y = x_ref[...] + (scale * gate_ref[0]) * acc
    for o_ref in o_refs:
        o_ref[...] = y.astype(o_ref.dtype)


def _matmul_residual(a_list, b_list, x2, mod3, seg_gate, scale, seq, tm, tn, out_dtypes, weight_buffers=None):
    m, n = x2.shape
    n_pairs = len(a_list)
    in_specs = []
    for a in a_list:
        in_specs.append(pl.BlockSpec((tm, a.shape[1]), lambda j, i: (i, 0)))
    operands = list(a_list)
    for w, row_block, rows in b_list:
        in_specs.append(pl.BlockSpec((rows, tn), lambda j, i, rb=row_block: (rb, j), pipeline_mode=weight_buffers))
        operands.append(w)
    in_specs.append(pl.BlockSpec((tm, tn), lambda j, i: (i, j)))
    in_specs.append(pl.BlockSpec((1, 1, tn), lambda j, i: ((i * tm) // seq, 0, seg_gate * (n // tn) + j)))
    return pl.pallas_call(
        functools.partial(_residual_kernel, n_pairs=n_pairs, scale=scale),
        grid=(n // tn, m // tm),
        in_specs=in_specs,
        out_specs=[pl.BlockSpec((tm, tn), lambda j, i: (i, j))] * len(out_dtypes),
        out_shape=[jax.ShapeDtypeStruct((m, n), dt) for dt in out_dtypes],
        compiler_params=_params("parallel", "parallel"),
        name="matmul_residual",
    )(*operands, x2, mod3)


def _proj_kernel(h_ref, w_ref, cos_ref, sin_ref, gn_g_ref, gn_b_ref, o_ref, *, ends, head_dim, group_dim, k_scale,
                 n_sub):
    j = pl.program_id(1)
    q_end, k_end, v_end, g_end, u_end = ends
    sub = h_ref.shape[0] // n_sub
    tn = w_ref.shape[1]

    def run(epilogue):
        for r in range(n_sub):
            rows = slice(r * sub, (r + 1) * sub)
            acc = jnp.dot(h_ref[rows, :], w_ref[...], preferred_element_type=F32)
            epilogue(acc, rows)

    def rotary(acc, rows):
        scale = jnp.where(j >= q_end, k_scale, 1.0).astype(F32)
        cos = cos_ref[rows, :] * scale
        sin = sin_ref[rows, :] * scale
        half = head_dim // 2
        for hd in range(tn // head_dim):
            lo = slice(hd * head_dim, hd * head_dim + half)
            hi = slice(hd * head_dim + half, (hd + 1) * head_dim)
            t1, t2 = acc[:, lo], acc[:, hi]
            o_ref[rows, lo] = (t1 * cos - t2 * sin).astype(o_ref.dtype)
            o_ref[rows, hi] = (t1 * sin + t2 * cos).astype(o_ref.dtype)

    def pointwise(act):
        def epilogue(acc, rows):
            o_ref[rows, :] = act(acc).astype(o_ref.dtype)
        return epilogue

    def gelu_groupnorm(acc, rows):
        for gi in range(tn // group_dim):
            cols = slice(gi * group_dim, (gi + 1) * group_dim)
            v = _gelu_tanh(acc[:, cols])
            mu = jnp.mean(v, axis=-1, keepdims=True)
            cen = v - mu
            var = jnp.mean(cen * cen, axis=-1, keepdims=True)
            o_ref[rows, cols] = (cen * lax.rsqrt(var + EPS) * gn_g_ref[:, cols] + gn_b_ref[:, cols]).astype(
                o_ref.dtype)

    pl.when(j < k_end)(lambda: run(rotary))
    pl.when((j >= k_end) & (j < v_end))(lambda: run(pointwise(lambda t: t)))
    pl.when((j >= v_end) & (j < g_end))(lambda: run(pointwise(_silu)))
    pl.when((j >= g_end) & (j < u_end))(lambda: run(pointwise(_gelu_tanh)))
    pl.when(j >= u_end)(lambda: run(gelu_groupnorm))


def _proj(h, w, cos, sin, gn_g, gn_b, ret_width, sgu_width, head_dim, group_dim, tm=1024, tn=1024, n_sub=2):
    m, k = h.shape
    n = w.shape[1]
    rb, sb = ret_width // tn, sgu_width // tn
    ends = (rb, 2 * rb, 3 * rb, 4 * rb, 4 * rb + sb)
    gn_spec = pl.BlockSpec((1, tn), lambda i, j: (0, jnp.clip(j - ends[4], 0, sb - 1)))
    rope_spec = pl.BlockSpec((tm, head_dim // 2), lambda i, j: (i, 0))
    return pl.pallas_call(
        functools.partial(_proj_kernel, ends=ends, head_dim=head_dim, group_dim=group_dim,
                          k_scale=head_dim ** -0.5, n_sub=n_sub),
        grid=(m // tm, n // tn),
        in_specs=[pl.BlockSpec((tm, k), lambda i, j: (i, 0)), pl.BlockSpec((k, tn), lambda i, j: (0, j)),
                  rope_spec, rope_spec, gn_spec, gn_spec],
        out_specs=pl.BlockSpec((tm, tn), lambda i, j: (i, j)),
        out_shape=jax.ShapeDtypeStruct((m, n), BF16),
        compiler_params=_params("parallel", "arbitrary"),
        name="proj",
    )(h, w, cos, sin, gn_g, gn_b)


def _retention_kernel(q_ref, k_ref, v_ref, g_ref, decay_ref, xi_ref, zeta_ref, cd_ref, o_ref, state_ref,
                      *, n_chunks, chunk):
    @pl.when(pl.program_id(2) == 0)
    def _():
        state_ref[...] = jnp.zeros_like(state_ref)

    decay = decay_ref[0]
    xi = xi_ref[0]
    zeta = zeta_ref[0]
    cd = cd_ref[0]
    for c in range(n_chunks):
        rows = pl.ds(c * chunk, chunk)
        q = q_ref[rows, :]
        k = k_ref[rows, :]
        v = v_ref[rows, :]
        state = state_ref[...]
        scores = lax.dot_general(q, k, (((1,), (1,)), ((), ())), preferred_element_type=F32) * decay
        intra = jnp.dot(scores.astype(BF16), v, preferred_element_type=F32)
        inter = jnp.dot(q, state.astype(BF16), preferred_element_type=F32) * xi
        kz = (k.astype(F32) * zeta).astype(BF16)
        kv = lax.dot_general(kz, v, (((0,), (0,)), ((), ())), preferred_element_type=F32)
        state_ref[...] = state * cd + kv
        out = intra + inter
        out = out * lax.rsqrt(jnp.mean(out * out, axis=-1, keepdims=True) + EPS)
        o_ref[rows, :] = (out * g_ref[rows, :].astype(F32)).astype(o_ref.dtype)


def _retention(proj, width, decay, xi, zeta, cd, batch, seq, n_chunks=8):
    m = proj.shape[0]
    dh = width // RET_HEADS
    chunk = decay.shape[1]
    rows = n_chunks * chunk
    steps = seq // rows
    row_block = lambda b, h, c: b * steps + c
    tok = lambda col0: pl.BlockSpec((rows, dh), lambda b, h, c: (row_block(b, h, c), col0 + h))
    return pl.pallas_call(
        functools.partial(_retention_kernel, n_chunks=n_chunks, chunk=chunk),
        grid=(batch, RET_HEADS, steps),
        in_specs=[
            tok(0), tok(RET_HEADS), tok(2 * RET_HEADS), tok(3 * RET_HEADS),
            pl.BlockSpec((1, chunk, chunk), lambda b, h, c: (h, 0, 0)),
            pl.BlockSpec((1, chunk, dh), lambda b, h, c: (h, 0, 0)),
            pl.BlockSpec((1, chunk, dh), lambda b, h, c: (h, 0, 0)),
            pl.BlockSpec((1, 1, dh), lambda b, h, c: (h, 0, 0)),
        ],
        out_specs=tok(0),
        out_shape=jax.ShapeDtypeStruct((m, width), BF16),
        scratch_shapes=[pltpu.VMEM((dh, dh), F32)],
        compiler_params=_params("parallel", "parallel", "arbitrary"),
        name="retention",
    )(proj, proj, proj, proj, decay, xi, zeta, cd)


def _sgu_kernel(u_ref, vg_ref, w_ref, b_ref, o_ref, *, n_chunks):
    row = lax.broadcasted_iota(jnp.int32, (CHUNK, CHUNK), 0)
    col = lax.broadcasted_iota(jnp.int32, (CHUNK, CHUNK), 1)
    w = (w_ref[0] * (row >= col).astype(F32)).astype(BF16)
    bias = b_ref[0]
    for c in range(n_chunks):
        rows = pl.ds(c * CHUNK, CHUNK)
        mixed = jnp.dot(w, vg_ref[rows, :], preferred_element_type=F32) + bias
        o_ref[rows, :] = (u_ref[rows, :].astype(F32) * mixed).astype(o_ref.dtype)


def _sgu(proj, col0, width, w_s, bias, n_chunks=16):
    m = proj.shape[0]
    dg = width // SGU_GROUPS
    rows = n_chunks * CHUNK
    tok_u = pl.BlockSpec((rows, dg), lambda i, g: (i, col0 // dg + g))
    tok_vg = pl.BlockSpec((rows, dg), lambda i, g: (i, col0 // dg + SGU_GROUPS + g))
    return pl.pallas_call(
        functools.partial(_sgu_kernel, n_chunks=n_chunks),
        grid=(m // rows, SGU_GROUPS),
        in_specs=[tok_u, tok_vg,
                  pl.BlockSpec((1, CHUNK, CHUNK), lambda i, g: (g, 0, 0)),
                  pl.BlockSpec((1, CHUNK, dg), lambda i, g: (g, 0, 0))],
        out_specs=pl.BlockSpec((rows, dg), lambda i, g: (i, g)),
        out_shape=jax.ShapeDtypeStruct((m, width), BF16),
        compiler_params=_params("parallel", "parallel"),
        name="sgu",
    )(proj, proj, w_s, bias)


def _retention_tables(dh, chunk):
    log_gamma = jnp.log(1.0 - 2.0 ** (-5.0 - jnp.arange(RET_HEADS, dtype=F32)))
    idx = jnp.arange(chunk)
    dist = (idx[:, None] - idx[None, :]).astype(F32)
    decay = jnp.where(dist[None] >= 0, jnp.exp(log_gamma[:, None, None] * jnp.maximum(dist, 0.0)[None]), 0.0)
    xi = jnp.exp(log_gamma[:, None] * (idx + 1).astype(F32))
    zeta = jnp.exp(log_gamma[:, None] * (chunk - 1 - idx).astype(F32))
    cd = jnp.exp(log_gamma * chunk)
    bcast = lambda t: jnp.broadcast_to(t[:, :, None], (RET_HEADS, chunk, dh))
    return decay, bcast(xi), bcast(zeta), jnp.broadcast_to(cd[:, None, None], (RET_HEADS, 1, dh))


def kernel(x, c, positions, ada_w, ada_b, norm_ffn1_g, ffn1_w1, ffn1_w3, ffn1_w2, norm_mix_g, w_in, sgu_norm_g, sgu_norm_b, sgu_w_s, sgu_b_s, w_out, norm_ffn2_g, ffn2_w1, ffn2_w3, ffn2_w2, final_norm_g):
    batch, seq, d = x.shape
    m = batch * seq
    depth = ada_w.shape[0]
    d_ff = ffn1_w2.shape[1]
    ret_width = w_out.shape[1] // 2
    sgu_width = w_out.shape[1] - ret_width
    dh = ret_width // RET_HEADS
    dg = sgu_width // SGU_GROUPS

    x2 = x.reshape(m, d)
    c_pad = jnp.zeros((8, d), F32).at[:batch].set(c)
    inv_freq = ROPE_BASE ** (-jnp.arange(0, dh // 2, dtype=F32) / (dh // 2))
    cos, sin = _rope_tables(positions.reshape(m, 1), inv_freq)
    decay, xi, zeta, cd = _retention_tables(dh, RET_CHUNK)

    for l in range(depth):
        mod3 = _ada_mod(c_pad, ada_w[l], ada_b[l])[:batch].reshape(batch, 1, N_MOD * d)

        h = _norm_mod(x2, norm_ffn1_g[l], mod3, 0, 1, seq)
        act, (w2_b, w_in_b, w_out_b, w1_b, w3_b) = _ffn_up(
            h, ffn1_w1[l], ffn1_w3[l],
            [(ffn1_w2[l], 32), (w_in[l], 16), (w_out[l], 16), (ffn2_w1[l], 16), (ffn2_w3[l], 16)], n_tiles=1)
        x2, xb = _matmul_residual([act], [(w2_b, 0, d_ff)], x2, mod3, 2, 0.5, seq, 512, 512, (F32, BF16))

        h = _norm_mod(xb, norm_mix_g[l], mod3, 3, 4, seq)
        proj = _proj(h, w_in_b, cos, sin, sgu_norm_g[l].reshape(1, sgu_width), sgu_norm_b[l].reshape(1, sgu_width),
                     ret_width, sgu_width, dh, dg)
        ret = _retention(proj, ret_width, decay, xi, zeta, cd, batch, seq)
        bias = jnp.broadcast_to(sgu_b_s[l][:, :, None], (SGU_GROUPS, CHUNK, dg))
        gated = _sgu(proj, 4 * ret_width, sgu_width, sgu_w_s[l], bias)
        x2, xb = _matmul_residual([ret, gated], [(w_out_b, 0, ret_width), (w_out_b, 1, sgu_width)], x2, mod3, 5, 1.0,
                                  seq, 1024, 1024, (F32, BF16), weight_buffers=pl.Buffered(1))

        h = _norm_mod(xb, norm_ffn2_g[l], mod3, 6, 7, seq)
        act, (w2_b,) = _ffn_up(h, w1_b, w3_b, [(ffn2_w2[l], 128)], n_tiles=2)
        last = l == depth - 1
        outs = _matmul_residual([act], [(w2_b, 0, d_ff)], x2, mod3, 8, 0.5, seq, 512, 512,
                                (BF16,) if last else (F32, BF16))
        x2, xb = (None, outs[0]) if last else outs

    return _final_norm(xb, final_norm_g).reshape(batch, seq, d)
```

```python
import functools

import jax
import jax.numpy as jnp
from jax import lax
from jax.experimental import pallas as pl
from jax.experimental.pallas import tpu as pltpu

F32 = jnp.float32
BF16 = jnp.bfloat16

RET_HEADS = 8
SGU_GROUPS = 8
CHUNK = 128
RET_CHUNK = 256
NORM_UNROLL = 8
ROPE_BASE = 10000.0
EPS = 1e-6
N_MOD = 9

V7X_VMEM_BYTES = 64 * 1024 * 1024
VMEM_LIMIT_BYTES = V7X_VMEM_BYTES - 3 * 1024 * 1024


def _params(*semantics):
    return pltpu.CompilerParams(dimension_semantics=semantics, vmem_limit_bytes=VMEM_LIMIT_BYTES)


def _silu(x):
    return x / (1.0 + jnp.exp(-x))


def _gelu_tanh(x):
    return 0.5 * x * (1.0 + jnp.tanh(0.7978845608028654 * (x + 0.044715 * (x * x * x))))


def _ada_kernel(c_ref, w_ref, b_ref, o_ref):
    sc = _silu(c_ref[...]).astype(BF16)
    o_ref[...] = jnp.dot(sc, w_ref[...].astype(BF16), preferred_element_type=F32) + b_ref[...]


def _ada_mod(c_pad, w, b, tn=512):
    rows, d = c_pad.shape
    n = w.shape[1]
    return pl.pallas_call(
        _ada_kernel,
        grid=(n // tn,),
        in_specs=[
            pl.BlockSpec((rows, d), lambda j: (0, 0)),
            pl.BlockSpec((d, tn), lambda j: (0, j)),
            pl.BlockSpec((1, tn), lambda j: (0, j)),
        ],
        out_specs=pl.BlockSpec((rows, tn), lambda j: (0, j)),
        out_shape=jax.ShapeDtypeStruct((rows, n), F32),
        compiler_params=_params("parallel"),
        name="ada_mod",
    )(c_pad, w, b.reshape(1, n))


def _norm_rows(x_ref, o_ref, finish, slab):
    def body(s, carry):
        rows = pl.ds(pl.multiple_of(s * slab, slab), slab)
        x = x_ref[rows, :].astype(F32)
        r = lax.rsqrt(jnp.mean(x * x, axis=-1, keepdims=True) + EPS)
        o_ref[rows, :] = finish(x * r).astype(o_ref.dtype)
        return carry

    lax.fori_loop(0, x_ref.shape[0] // slab, body, 0, unroll=NORM_UNROLL)


def _norm_mod_kernel(x_ref, g_ref, sh_ref, sc_ref, o_ref, *, slab):
    gain = g_ref[...] * (1.0 + sc_ref[0])
    shift = sh_ref[0]
    _norm_rows(x_ref, o_ref, lambda y: y * gain + shift, slab)


def _norm_kernel(x_ref, g_ref, o_ref, *, slab):
    g = g_ref[...]
    _norm_rows(x_ref, o_ref, lambda y: y * g, slab)


def _norm_slab(dtype):
    return 8 * (4 // jnp.dtype(dtype).itemsize)


def _norm_mod(x2, g, mod3, seg_shift, seg_scale, seq, tm=512):
    m, d = x2.shape
    bidx = lambda i: (i * tm) // seq
    return pl.pallas_call(
        functools.partial(_norm_mod_kernel, slab=_norm_slab(x2.dtype)),
        grid=(m // tm,),
        in_specs=[
            pl.BlockSpec((tm, d), lambda i: (i, 0)),
            pl.BlockSpec((1, d), lambda i: (0, 0)),
            pl.BlockSpec((1, 1, d), lambda i: (bidx(i), 0, seg_shift)),
            pl.BlockSpec((1, 1, d), lambda i: (bidx(i), 0, seg_scale)),
        ],
        out_specs=pl.BlockSpec((tm, d), lambda i: (i, 0)),
        out_shape=jax.ShapeDtypeStruct((m, d), BF16),
        compiler_params=_params("parallel"),
        name="norm_mod",
    )(x2, g.reshape(1, d), mod3, mod3)


def _final_norm(x2, g, tm=512):
    m, d = x2.shape
    return pl.pallas_call(
        functools.partial(_norm_kernel, slab=_norm_slab(x2.dtype)),
        grid=(m // tm,),
        in_specs=[pl.BlockSpec((tm, d), lambda i: (i, 0)), pl.BlockSpec((1, d), lambda i: (0, 0))],
        out_specs=pl.BlockSpec((tm, d), lambda i: (i, 0)),
        out_shape=jax.ShapeDtypeStruct((m, d), F32),
        compiler_params=_params("parallel"),
        name="final_norm",
    )(x2, g.reshape(1, d))


def _rope_kernel(pos_ref, freq_ref, cos_ref, sin_ref):
    ang = pos_ref[...].astype(F32) * freq_ref[...]
    cos_ref[...] = jnp.cos(ang)
    sin_ref[...] = jnp.sin(ang)


def _rope_tables(pos_col, inv_freq, tm=1024):
    m = pos_col.shape[0]
    half = inv_freq.shape[0]
    spec = pl.BlockSpec((tm, half), lambda i: (i, 0))
    return pl.pallas_call(
        _rope_kernel,
        grid=(m // tm,),
        in_specs=[pl.BlockSpec((tm, 1), lambda i: (i, 0)), pl.BlockSpec((1, half), lambda i: (0, 0))],
        out_specs=[spec, spec],
        out_shape=[jax.ShapeDtypeStruct((m, half), F32)] * 2,
        compiler_params=_params("parallel"),
        name="rope_tables",
    )(pos_col, inv_freq.reshape(1, half))


def _as_bf16(ref):
    w = ref[...]
    return w if w.dtype == BF16 else w.astype(BF16)


def _ffn_up_kernel(*refs, n_tiles, n_side, last_tile):
    h_ref = refs[0]
    w1_refs = refs[1:1 + n_tiles]
    w3_refs = refs[1 + n_tiles:1 + 2 * n_tiles]
    side_in = refs[1 + 2 * n_tiles:1 + 2 * n_tiles + n_side]
    o_ref = refs[1 + 2 * n_tiles + n_side]
    side_out = refs[2 + 2 * n_tiles + n_side:]
    tn = w1_refs[0].shape[1]

    def tile(t):
        h = h_ref[...]
        a = jnp.dot(h, _as_bf16(w1_refs[t]), preferred_element_type=F32)
        b = jnp.dot(h, _as_bf16(w3_refs[t]), preferred_element_type=F32)
        o_ref[:, t * tn:(t + 1) * tn] = (_silu(a) * b).astype(o_ref.dtype)

    if n_tiles == 1:
        tile(0)
    else:
        first = pl.program_id(1) * n_tiles

        @pl.when(first + n_tiles - 1 <= last_tile)
        def _():
            for t in range(n_tiles):
                tile(t)

        @pl.when(first + n_tiles - 1 > last_tile)
        def _():
            for t in range(n_tiles):
                if_valid = first + t <= last_tile
                pl.when(if_valid)(functools.partial(tile, t))

                @pl.when(jnp.logical_not(if_valid))
                def _():
                    o_ref[:, t * tn:(t + 1) * tn] = jnp.zeros((o_ref.shape[0], tn), o_ref.dtype)

    for src, dst in zip(side_in, side_out):
        dst[...] = src[...].astype(dst.dtype)


def _ffn_up(h, w1, w3, side, n_tiles, tm=2048, tn=256):
    m, k = h.shape
    f = w1.shape[1]
    last_tile = f // tn - 1
    steps_j = pl.cdiv(last_tile + 1, n_tiles)
    walk = lambda nb: (lambda i, j: (jnp.minimum(i * steps_j + j, nb - 1), 0))
    wspecs = [pl.BlockSpec((k, tn), lambda i, j, t=t: (0, jnp.minimum(j * n_tiles + t, last_tile)))
              for t in range(n_tiles)]
    side_specs, side_shapes = [], []
    for w, rb in side:
        nb = w.shape[0] // rb
        assert nb * rb == w.shape[0] and nb <= (m // tm) * steps_j
        side_specs.append(pl.BlockSpec((rb, w.shape[1]), walk(nb)))
        side_shapes.append(jax.ShapeDtypeStruct(w.shape, BF16))
    outs = pl.pallas_call(
        functools.partial(_ffn_up_kernel, n_tiles=n_tiles, n_side=len(side), last_tile=last_tile),
        grid=(m // tm, steps_j),
        in_specs=[pl.BlockSpec((tm, k), lambda i, j: (i, 0), pipeline_mode=pl.Buffered(1))]
        + wspecs + wspecs + side_specs,
        out_specs=[pl.BlockSpec((tm, n_tiles * tn), lambda i, j: (i, j))] + side_specs,
        out_shape=[jax.ShapeDtypeStruct((m, steps_j * n_tiles * tn), BF16)] + side_shapes,
        compiler_params=_params("arbitrary", "arbitrary"),
        name="ffn_up",
    )(h, *([w1] * n_tiles), *([w3] * n_tiles), *[w for w, _ in side])
    return outs[0], outs[1:]


def _residual_kernel(*refs, n_pairs, scale):
    a_refs = refs[:n_pairs]
    b_refs = refs[n_pairs:2 * n_pairs]
    x_ref, gate_ref = refs[2 * n_pairs:2 * n_pairs + 2]
    o_refs = refs[2 * n_pairs + 2:]
    acc = None
    for a_ref, b_ref in zip(a_refs, b_refs):
        part = jnp.dot(a_ref[:, :b_ref.shape[0]], b_ref[...], preferred_element_type=F32)
        acc = part if acc is None else acc + part
    y = x_ref[...] + (scale * gate_ref[0]) * acc
    for o_ref in o_refs:
        o_ref[...] = y.astype(o_ref.dtype)


def _matmul_residual(a_list, b_list, x2, mod3, seg_gate, scale, seq, tm, tn, out_dtypes):
    m, n = x2.shape
    n_pairs = len(a_list)
    in_specs = []
    for a in a_list:
        in_specs.append(pl.BlockSpec((tm, a.shape[1]), lambda j, i: (i, 0)))
    operands = list(a_list)
    for w, row_block, rows in b_list:
        in_specs.append(pl.BlockSpec((rows, tn), lambda j, i, rb=row_block: (rb, j)))
        operands.append(w)
    in_specs.append(pl.BlockSpec((tm, tn), lambda j, i: (i, j)))
    in_specs.append(pl.BlockSpec((1, 1, tn), lambda j, i: ((i * tm) // seq, 0, seg_gate * (n // tn) + j)))
    return pl.pallas_call(
        functools.partial(_residual_kernel, n_pairs=n_pairs, scale=scale),
        grid=(n // tn, m // tm),
        in_specs=in_specs,
        out_specs=[pl.BlockSpec((tm, tn), lambda j, i: (i, j))] * len(out_dtypes),
        out_shape=[jax.ShapeDtypeStruct((m, n), dt) for dt in out_dtypes],
        compiler_params=_params("parallel", "parallel"),
        name="matmul_residual",
    )(*operands, x2, mod3)


def _proj_kernel(h_ref, w_ref, cos_ref, sin_ref, gn_g_ref, gn_b_ref, o_ref, *, ends, head_dim, group_dim, k_scale):
    j = pl.program_id(1)
    q_end, k_end, v_end, g_end, u_end = ends
    tn = w_ref.shape[1]

    def run(epilogue, n_sub):
        sub = h_ref.shape[0] // n_sub
        for r in range(n_sub):
            rows = slice(r * sub, (r + 1) * sub)
            acc = jnp.dot(h_ref[rows, :], w_ref[...], preferred_element_type=F32)
            epilogue(acc, rows)

    def rotary(acc, rows):
        scale = jnp.where(j >= q_end, k_scale, 1.0).astype(F32)
        cos = cos_ref[rows, :] * scale
        sin = sin_ref[rows, :] * scale
        half = head_dim // 2
        for hd in range(tn // head_dim):
            lo = slice(hd * head_dim, hd * head_dim + half)
            hi = slice(hd * head_dim + half, (hd + 1) * head_dim)
            t1, t2 = acc[:, lo], acc[:, hi]
            o_ref[rows, lo] = (t1 * cos - t2 * sin).astype(o_ref.dtype)
            o_ref[rows, hi] = (t1 * sin + t2 * cos).astype(o_ref.dtype)

    def pointwise(act):
        def epilogue(acc, rows):
            o_ref[rows, :] = act(acc).astype(o_ref.dtype)
        return epilogue

    def gelu_groupnorm(acc, rows):
        for gi in range(tn // group_dim):
            cols = slice(gi * group_dim, (gi + 1) * group_dim)
            v = _gelu_tanh(acc[:, cols])
            mu = jnp.mean(v, axis=-1, keepdims=True)
            cen = v - mu
            var = jnp.mean(cen * cen, axis=-1, keepdims=True)
            o_ref[rows, cols] = (cen * lax.rsqrt(var + EPS) * gn_g_ref[:, cols] + gn_b_ref[:, cols]).astype(
                o_ref.dtype)

    pl.when(j < k_end)(lambda: run(rotary, 1))
    pl.when((j >= k_end) & (j < v_end))(lambda: run(pointwise(lambda t: t), 1))
    pl.when((j >= v_end) & (j < g_end))(lambda: run(pointwise(_silu), 2))
    pl.when((j >= g_end) & (j < u_end))(lambda: run(pointwise(_gelu_tanh), 4))
    pl.when(j >= u_end)(lambda: run(gelu_groupnorm, 4))


def _proj(h, w, cos, sin, gn_g, gn_b, ret_width, sgu_width, head_dim, group_dim, tm=1024, tn=1024):
    m, k = h.shape
    n = w.shape[1]
    rb, sb = ret_width // tn, sgu_width // tn
    ends = (rb, 2 * rb, 3 * rb, 4 * rb, 4 * rb + sb)
    gn_spec = pl.BlockSpec((1, tn), lambda i, j: (0, jnp.clip(j - ends[4], 0, sb - 1)))
    rope_spec = pl.BlockSpec((tm, head_dim // 2), lambda i, j: (i, 0))
    return pl.pallas_call(
        functools.partial(_proj_kernel, ends=ends, head_dim=head_dim, group_dim=group_dim,
                          k_scale=head_dim ** -0.5),
        grid=(m // tm, n // tn),
        in_specs=[pl.BlockSpec((tm, k), lambda i, j: (i, 0)), pl.BlockSpec((k, tn), lambda i, j: (0, j)),
                  rope_spec, rope_spec, gn_spec, gn_spec],
        out_specs=pl.BlockSpec((tm, tn), lambda i, j: (i, j)),
        out_shape=jax.ShapeDtypeStruct((m, n), BF16),
        compiler_params=_params("parallel", "arbitrary"),
        name="proj",
    )(h, w, cos, sin, gn_g, gn_b)


def _retention_kernel(q_ref, k_ref, v_ref, g_ref, decay_ref, xi_ref, zeta_ref, cd_ref, o_ref, state_ref,
                      *, n_chunks, chunk):
    @pl.when(pl.program_id(1) == 0)
    def _():
        state_ref[...] = jnp.zeros_like(state_ref)

    dh = state_ref.shape[1]
    for h in range(RET_HEADS):
        cols = slice(h * dh, (h + 1) * dh)
        for c in range(n_chunks):
            rows = slice(c * chunk, (c + 1) * chunk)
            q = q_ref[rows, cols]
            k = k_ref[rows, cols]
            v = v_ref[rows, cols]
            state = state_ref[h]
            scores = lax.dot_general(q, k, (((1,), (1,)), ((), ())), preferred_element_type=F32) * decay_ref[h]
            intra = jnp.dot(scores.astype(BF16), v, preferred_element_type=F32)
            inter = jnp.dot(q, state.astype(BF16), preferred_element_type=F32) * xi_ref[h]
            kz = (k.astype(F32) * zeta_ref[h]).astype(BF16)
            kv = lax.dot_general(kz, v, (((0,), (0,)), ((), ())), preferred_element_type=F32)
            state_ref[h] = state * cd_ref[h] + kv
            out = intra + inter
            out = out * lax.rsqrt(jnp.mean(out * out, axis=-1, keepdims=True) + EPS)
            o_ref[rows, cols] = (out * g_ref[rows, cols].astype(F32)).astype(o_ref.dtype)


def _retention(proj, width, decay, xi, zeta, cd, batch, seq, n_chunks=2):
    m = proj.shape[0]
    dh = width // RET_HEADS
    chunk = decay.shape[1]
    rows = n_chunks * chunk
    steps = seq // rows
    tok = lambda part: pl.BlockSpec((rows, width), lambda b, c: (b * steps + c, part))
    table = lambda t: pl.BlockSpec(t.shape, lambda b, c: (0, 0, 0))
    return pl.pallas_call(
        functools.partial(_retention_kernel, n_chunks=n_chunks, chunk=chunk),
        grid=(batch, steps),
        in_specs=[tok(0), tok(1), tok(2), tok(3), table(decay), table(xi), table(zeta), table(cd)],
        out_specs=tok(0),
        out_shape=jax.ShapeDtypeStruct((m, width), BF16),
        scratch_shapes=[pltpu.VMEM((RET_HEADS, dh, dh), F32)],
        compiler_params=_params("parallel", "arbitrary"),
        name="retention",
    )(proj, proj, proj, proj, decay, xi, zeta, cd)


def _sgu_kernel(u_ref, vg_ref, w_ref, b_ref, o_ref, *, n_chunks):
    row = lax.broadcasted_iota(jnp.int32, (CHUNK, CHUNK), 0)
    col = lax.broadcasted_iota(jnp.int32, (CHUNK, CHUNK), 1)
    causal = (row >= col).astype(F32)
    dg = b_ref.shape[2]
    for g in range(SGU_GROUPS):
        cols = slice(g * dg, (g + 1) * dg)
        w = (w_ref[g] * causal).astype(BF16)
        bias = b_ref[g]
        for c in range(n_chunks):
            rows = slice(c * CHUNK, (c + 1) * CHUNK)
            mixed = jnp.dot(w, vg_ref[rows, cols], preferred_element_type=F32) + bias
            o_ref[rows, cols] = (u_ref[rows, cols].astype(F32) * mixed).astype(o_ref.dtype)


def _sgu(proj, col0, width, w_s, bias, n_chunks=4):
    m = proj.shape[0]
    rows = n_chunks * CHUNK
    return pl.pallas_call(
        functools.partial(_sgu_kernel, n_chunks=n_chunks),
        grid=(m // rows,),
        in_specs=[pl.BlockSpec((rows, width), lambda i: (i, col0 // width)),
                  pl.BlockSpec((rows, width), lambda i: (i, col0 // width + 1)),
                  pl.BlockSpec(w_s.shape, lambda i: (0, 0, 0)),
                  pl.BlockSpec(bias.shape, lambda i: (0, 0, 0))],
        out_specs=pl.BlockSpec((rows, width), lambda i: (i, 0)),
        out_shape=jax.ShapeDtypeStruct((m, width), BF16),
        compiler_params=_params("parallel"),
        name="sgu",
    )(proj, proj, w_s, bias)


def _retention_tables(dh, chunk):
    log_gamma = jnp.log(1.0 - 2.0 ** (-5.0 - jnp.arange(RET_HEADS, dtype=F32)))
    idx = jnp.arange(chunk)
    dist = (idx[:, None] - idx[None, :]).astype(F32)
    decay = jnp.where(dist[None] >= 0, jnp.exp(log_gamma[:, None, None] * jnp.maximum(dist, 0.0)[None]), 0.0)
    xi = jnp.exp(log_gamma[:, None] * (idx + 1).astype(F32))
    zeta = jnp.exp(log_gamma[:, None] * (chunk - 1 - idx).astype(F32))
    cd = jnp.exp(log_gamma * chunk)
    bcast = lambda t: jnp.broadcast_to(t[:, :, None], (RET_HEADS, chunk, dh))
    return decay, bcast(xi), bcast(zeta), jnp.broadcast_to(cd[:, None, None], (RET_HEADS, 1, dh))


def kernel(x, c, positions, ada_w, ada_b, norm_ffn1_g, ffn1_w1, ffn1_w3, ffn1_w2, norm_mix_g, w_in, sgu_norm_g, sgu_norm_b, sgu_w_s, sgu_b_s, w_out, norm_ffn2_g, ffn2_w1, ffn2_w3, ffn2_w2, final_norm_g):
    batch, seq, d = x.shape
    m = batch * seq
    depth = ada_w.shape[0]
    d_ff = ffn1_w2.shape[1]
    ret_width = w_out.shape[1] // 2
    sgu_width = w_out.shape[1] - ret_width
    dh = ret_width // RET_HEADS
    dg = sgu_width // SGU_GROUPS

    x2 = x.reshape(m, d)
    c_pad = jnp.zeros((8, d), F32).at[:batch].set(c)
    inv_freq = ROPE_BASE ** (-jnp.arange(0, dh // 2, dtype=F32) / (dh // 2))
    cos, sin = _rope_tables(positions.reshape(m, 1), inv_freq)
    decay, xi, zeta, cd = _retention_tables(dh, RET_CHUNK)

    for l in range(depth):
        mod3 = _ada_mod(c_pad, ada_w[l], ada_b[l])[:batch].reshape(batch, 1, N_MOD * d)

        h = _norm_mod(x2, norm_ffn1_g[l], mod3, 0, 1, seq)
        act, (w2_b, w_in_b, w_out_b, w1_b, w3_b) = _ffn_up(
            h, ffn1_w1[l], ffn1_w3[l],
            [(ffn1_w2[l], 32), (w_in[l], 16), (w_out[l], 16), (ffn2_w1[l], 16), (ffn2_w3[l], 16)], n_tiles=1)
        x2, xb = _matmul_residual([act], [(w2_b, 0, d_ff)], x2, mod3, 2, 0.5, seq, 512, 512, (F32, BF16))

        h = _norm_mod(xb, norm_mix_g[l], mod3, 3, 4, seq)
        proj = _proj(h, w_in_b, cos, sin, sgu_norm_g[l].reshape(1, sgu_width), sgu_norm_b[l].reshape(1, sgu_width),
                     ret_width, sgu_width, dh, dg)
        ret = _retention(proj, ret_width, decay, xi, zeta, cd, batch, seq)
        bias = jnp.broadcast_to(sgu_b_s[l][:, :, None], (SGU_GROUPS, CHUNK, dg))
        gated = _sgu(proj, 4 * ret_width, sgu_width, sgu_w_s[l], bias)
        x2, xb = _matmul_residual([ret, gated], [(w_out_b, 0, ret_width), (w_out_b, 1, sgu_width)], x2, mod3, 5, 1.0,
                                  seq, 1024, 1024, (F32, BF16))

        h = _norm_mod(xb, norm_ffn2_g[l], mod3, 6, 7, seq)
        act, (w2_b,) = _ffn_up(h, w1_b, w3_b, [(ffn2_w2[l], 128)], n_tiles=2)
        last = l == depth - 1
        outs = _matmul_residual([act], [(w2_b, 0, d_ff)], x2, mod3, 8, 0.5, seq, 512, 512,
                                (BF16,) if last else (F32, BF16))
        x2, xb = (None, outs[0]) if last else outs

    return _final_norm(xb, final_norm_g).reshape(batch, seq, d)
```

```python
import functools

import jax
import jax.numpy as jnp
from jax import lax
from jax.experimental import pallas as pl
from jax.experimental.pallas import tpu as pltpu

F32 = jnp.float32
BF16 = jnp.bfloat16

RET_HEADS = 8
SGU_GROUPS = 8
CHUNK = 128
RET_CHUNK = 256
NORM_SLAB = 8
NORM_UNROLL = 8
PROJ_SUB_TILES = 2
ROPE_BASE = 10000.0
EPS = 1e-6
N_MOD = 9

V7X_VMEM_BYTES = 64 * 1024 * 1024
VMEM_LIMIT_BYTES = V7X_VMEM_BYTES - 3 * 1024 * 1024


def _params(*semantics):
    return pltpu.CompilerParams(dimension_semantics=semantics, vmem_limit_bytes=VMEM_LIMIT_BYTES)


def _silu(x):
    half = 0.5 * x
    return half + half * jnp.tanh(half)


def _gelu_tanh(x):
    return 0.5 * x * (1.0 + jnp.tanh(0.7978845608028654 * (x + 0.044715 * (x * x * x))))


def _ada_kernel(c_ref, w_ref, b_ref, o_ref):
    sc = _silu(c_ref[...]).astype(BF16)
    o_ref[...] = jnp.dot(sc, w_ref[...].astype(BF16), preferred_element_type=F32) + b_ref[...]


def _ada_mod(c_pad, w, b, tn=512):
    rows, d = c_pad.shape
    n = w.shape[1]
    return pl.pallas_call(
        _ada_kernel,
        grid=(n // tn,),
        in_specs=[
            pl.BlockSpec((rows, d), lambda j: (0, 0)),
            pl.BlockSpec((d, tn), lambda j: (0, j)),
            pl.BlockSpec((1, tn), lambda j: (0, j)),
        ],
        out_specs=pl.BlockSpec((rows, tn), lambda j: (0, j)),
        out_shape=jax.ShapeDtypeStruct((rows, n), F32),
        compiler_params=_params("parallel"),
        name="ada_mod",
    )(c_pad, w, b.reshape(1, n))


def _norm_rows(x_ref, o_ref, finish, slab):
    def body(s, carry):
        rows = pl.ds(pl.multiple_of(s * slab, slab), slab)
        x = x_ref[rows, :].astype(F32)
        r = lax.rsqrt(jnp.mean(x * x, axis=-1, keepdims=True) + EPS)
        o_ref[rows, :] = finish(x * r).astype(o_ref.dtype)
        return carry

    lax.fori_loop(0, x_ref.shape[0] // slab, body, 0, unroll=NORM_UNROLL)


def _norm_mod_kernel(x_ref, g_ref, sh_ref, sc_ref, o_ref, *, slab):
    gain = g_ref[...] * (1.0 + sc_ref[0])
    shift = sh_ref[0]
    _norm_rows(x_ref, o_ref, lambda y: y * gain + shift, slab)


def _norm_kernel(x_ref, g_ref, o_ref, *, slab):
    g = g_ref[...]
    _norm_rows(x_ref, o_ref, lambda y: y * g, slab)


def _norm_mod(x2, g, mod3, seg_shift, seg_scale, seq, tm=1024):
    m, d = x2.shape
    bidx = lambda i: (i * tm) // seq
    return pl.pallas_call(
        functools.partial(_norm_mod_kernel, slab=NORM_SLAB),
        grid=(m // tm,),
        in_specs=[
            pl.BlockSpec((tm, d), lambda i: (i, 0)),
            pl.BlockSpec((1, d), lambda i: (0, 0)),
            pl.BlockSpec((1, 1, d), lambda i: (bidx(i), 0, seg_shift)),
            pl.BlockSpec((1, 1, d), lambda i: (bidx(i), 0, seg_scale)),
        ],
        out_specs=pl.BlockSpec((tm, d), lambda i: (i, 0)),
        out_shape=jax.ShapeDtypeStruct((m, d), BF16),
        compiler_params=_params("parallel"),
        name="norm_mod",
    )(x2, g.reshape(1, d), mod3, mod3)


def _final_norm(x2, g, tm=1024):
    m, d = x2.shape
    return pl.pallas_call(
        functools.partial(_norm_kernel, slab=NORM_SLAB),
        grid=(m // tm,),
        in_specs=[pl.BlockSpec((tm, d), lambda i: (i, 0)), pl.BlockSpec((1, d), lambda i: (0, 0))],
        out_specs=pl.BlockSpec((tm, d), lambda i: (i, 0)),
        out_shape=jax.ShapeDtypeStruct((m, d), F32),
        compiler_params=_params("parallel"),
        name="final_norm",
    )(x2, g.reshape(1, d))


def _rope_kernel(pos_ref, freq_ref, cos_ref, sin_ref):
    ang = pos_ref[...].astype(F32) * freq_ref[...]
    cos_ref[...] = jnp.cos(ang)
    sin_ref[...] = jnp.sin(ang)


def _rope_tables(pos_col, inv_freq, tm=1024):
    m = pos_col.shape[0]
    half = inv_freq.shape[0]
    spec = pl.BlockSpec((tm, half), lambda i: (i, 0))
    return pl.pallas_call(
        _rope_kernel,
        grid=(m // tm,),
        in_specs=[pl.BlockSpec((tm, 1), lambda i: (i, 0)), pl.BlockSpec((1, half), lambda i: (0, 0))],
        out_specs=[spec, spec],
        out_shape=[jax.ShapeDtypeStruct((m, half), F32)] * 2,
        compiler_params=_params("parallel"),
        name="rope_tables",
    )(pos_col, inv_freq.reshape(1, half))


def _as_bf16(ref):
    w = ref[...]
    return w if w.dtype == BF16 else w.astype(BF16)


def _ffn_up_kernel(*refs, n_tiles, n_side, last_tile):
    h_ref = refs[0]
    w1_refs = refs[1:1 + n_tiles]
    w3_refs = refs[1 + n_tiles:1 + 2 * n_tiles]
    side_in = refs[1 + 2 * n_tiles:1 + 2 * n_tiles + n_side]
    o_ref = refs[1 + 2 * n_tiles + n_side]
    side_out = refs[2 + 2 * n_tiles + n_side:]
    tn = w1_refs[0].shape[1]

    def tile(t):
        h = h_ref[...]
        a = jnp.dot(h, _as_bf16(w1_refs[t]), preferred_element_type=F32)
        b = jnp.dot(h, _as_bf16(w3_refs[t]), preferred_element_type=F32)
        o_ref[:, t * tn:(t + 1) * tn] = (_silu(a) * b).astype(o_ref.dtype)

    if n_tiles == 1:
        tile(0)
    else:
        first = pl.program_id(1) * n_tiles

        @pl.when(first + n_tiles - 1 <= last_tile)
        def _():
            for t in range(n_tiles):
                tile(t)

        @pl.when(first + n_tiles - 1 > last_tile)
        def _():
            for t in range(n_tiles):
                if_valid = first + t <= last_tile
                pl.when(if_valid)(functools.partial(tile, t))

                @pl.when(jnp.logical_not(if_valid))
                def _():
                    o_ref[:, t * tn:(t + 1) * tn] = jnp.zeros((o_ref.shape[0], tn), o_ref.dtype)

    for src, dst in zip(side_in, side_out):
        dst[...] = src[...].astype(dst.dtype)


def _ffn_up(h, w1, w3, side, n_tiles, tm=2048, tn=256):
    m, k = h.shape
    f = w1.shape[1]
    last_tile = f // tn - 1
    steps_j = pl.cdiv(last_tile + 1, n_tiles)
    walk = lambda nb: (lambda i, j: (jnp.minimum(i * steps_j + j, nb - 1), 0))
    wspecs = [pl.BlockSpec((k, tn), lambda i, j, t=t: (0, jnp.minimum(j * n_tiles + t, last_tile)))
              for t in range(n_tiles)]
    side_specs, side_shapes = [], []
    for w, rb in side:
        nb = w.shape[0] // rb
        assert nb * rb == w.shape[0] and nb <= (m // tm) * steps_j
        side_specs.append(pl.BlockSpec((rb, w.shape[1]), walk(nb)))
        side_shapes.append(jax.ShapeDtypeStruct(w.shape, BF16))
    outs = pl.pallas_call(
        functools.partial(_ffn_up_kernel, n_tiles=n_tiles, n_side=len(side), last_tile=last_tile),
        grid=(m // tm, steps_j),
        in_specs=[pl.BlockSpec((tm, k), lambda i, j: (i, 0), pipeline_mode=pl.Buffered(1))]
        + wspecs + wspecs + side_specs,
        out_specs=[pl.BlockSpec((tm, n_tiles * tn), lambda i, j: (i, j))] + side_specs,
        out_shape=[jax.ShapeDtypeStruct((m, steps_j * n_tiles * tn), BF16)] + side_shapes,
        compiler_params=_params("arbitrary", "arbitrary"),
        name="ffn_up",
    )(h, *([w1] * n_tiles), *([w3] * n_tiles), *[w for w, _ in side])
    return outs[0], outs[1:]


def _residual_kernel(*refs, n_pairs, scale):
    a_refs = refs[:n_pairs]
    b_refs = refs[n_pairs:2 * n_pairs]
    x_ref, gate_ref = refs[2 * n_pairs:2 * n_pairs + 2]
    o_refs = refs[2 * n_pairs + 2:]
    acc = None
    for a_ref, b_ref in zip(a_refs, b_refs):
        part = jnp.dot(a_ref[:, :b_ref.shape[0]], b_ref[...], preferred_element_type=F32)
        acc = part if acc is None else acc + part
    y = x_ref[...] + (scale * gate_ref[0]) * acc
    for o_ref in o_refs:
        o_ref[...] = y.astype(o_ref.dtype)


def _matmul_residual(a_list, b_list, x2, mod3, seg_gate, scale, seq, tm, tn, out_dtypes):
    m, n = x2.shape
    n_pairs = len(a_list)
    in_specs = []
    for a in a_list:
        in_specs.append(pl.BlockSpec((tm, a.shape[1]), lambda j, i: (i, 0)))
    operands = list(a_list)
    for w, row_block, rows in b_list:
        in_specs.append(pl.BlockSpec((rows, tn), lambda j, i, rb=row_block: (rb, j)))
        operands.append(w)
    in_specs.append(pl.BlockSpec((tm, tn), lambda j, i: (i, j)))
    in_specs.append(pl.BlockSpec((1, 1, tn), lambda j, i: ((i * tm) // seq, 0, seg_gate * (n // tn) + j)))
    return pl.pallas_call(
        functools.partial(_residual_kernel, n_pairs=n_pairs, scale=scale),
        grid=(n // tn, m // tm),
        in_specs=in_specs,
        out_specs=[pl.BlockSpec((tm, tn), lambda j, i: (i, j))] * len(out_dtypes),
        out_shape=[jax.ShapeDtypeStruct((m, n), dt) for dt in out_dtypes],
        compiler_params=_params("parallel", "parallel"),
        name="matmul_residual",
    )(*operands, x2, mod3)


def _proj_kernel(h_ref, w_ref, cos_ref, sin_ref, gn_g_ref, gn_b_ref, o_ref, *, ends, head_dim, group_dim, k_scale):
    j = pl.program_id(1)
    q_end, k_end, v_end, g_end, u_end = ends
    tn = w_ref.shape[1]

    def run(epilogue):
        sub = h_ref.shape[0] // PROJ_SUB_TILES
        for r in range(PROJ_SUB_TILES):
            rows = slice(r * sub, (r + 1) * sub)
            acc = jnp.dot(h_ref[rows, :], w_ref[...], preferred_element_type=F32)
            epilogue(acc, rows)

    def rotary(acc, rows):
        scale = jnp.where(j >= q_end, k_scale, 1.0).astype(F32)
        cos = cos_ref[rows, :] * scale
        sin = sin_ref[rows, :] * scale
        half = head_dim // 2
        for hd in range(tn // head_dim):
            lo = slice(hd * head_dim, hd * head_dim + half)
            hi = slice(hd * head_dim + half, (hd + 1) * head_dim)
            t1, t2 = acc[:, lo], acc[:, hi]
            o_ref[rows, lo] = (t1 * cos - t2 * sin).astype(o_ref.dtype)
            o_ref[rows, hi] = (t1 * sin + t2 * cos).astype(o_ref.dtype)

    def pointwise(act):
        def epilogue(acc, rows):
            o_ref[rows, :] = act(acc).astype(o_ref.dtype)
        return epilogue

    def gelu_groupnorm(acc, rows):
        for gi in range(tn // group_dim):
            cols = slice(gi * group_dim, (gi + 1) * group_dim)
            v = _gelu_tanh(acc[:, cols])
            mu = jnp.mean(v, axis=-1, keepdims=True)
            cen = v - mu
            var = jnp.mean(cen * cen, axis=-1, keepdims=True)
            o_ref[rows, cols] = (cen * lax.rsqrt(var + EPS) * gn_g_ref[:, cols] + gn_b_ref[:, cols]).astype(
                o_ref.dtype)

    pl.when(j < k_end)(lambda: run(rotary))
    pl.when((j >= k_end) & (j < v_end))(lambda: run(pointwise(lambda t: t)))
    pl.when((j >= v_end) & (j < g_end))(lambda: run(pointwise(_silu)))
    pl.when((j >= g_end) & (j < u_end))(lambda: run(pointwise(_gelu_tanh)))
    pl.when(j >= u_end)(lambda: run(gelu_groupnorm))


def _proj(h, w, cos, sin, gn_g, gn_b, ret_width, sgu_width, head_dim, group_dim, tm=1024, tn=1024):
    m, k = h.shape
    n = w.shape[1]
    rb, sb = ret_width // tn, sgu_width // tn
    ends = (rb, 2 * rb, 3 * rb, 4 * rb, 4 * rb + sb)
    gn_spec = pl.BlockSpec((1, tn), lambda i, j: (0, jnp.clip(j - ends[4], 0, sb - 1)))
    rope_spec = pl.BlockSpec((tm, head_dim // 2), lambda i, j: (i, 0))
    return pl.pallas_call(
        functools.partial(_proj_kernel, ends=ends, head_dim=head_dim, group_dim=group_dim,
                          k_scale=head_dim ** -0.5),
        grid=(m // tm, n // tn),
        in_specs=[pl.BlockSpec((tm, k), lambda i, j: (i, 0)), pl.BlockSpec((k, tn), lambda i, j: (0, j)),
                  rope_spec, rope_spec, gn_spec, gn_spec],
        out_specs=pl.BlockSpec((tm, tn), lambda i, j: (i, j)),
        out_shape=jax.ShapeDtypeStruct((m, n), BF16),
        compiler_params=_params("parallel", "arbitrary"),
        name="proj",
    )(h, w, cos, sin, gn_g, gn_b)


def _retention_kernel(q_ref, k_ref, v_ref, g_ref, decay_ref, xi_ref, zeta_ref, cd_ref, o_ref, state_ref,
                      *, n_chunks, chunk):
    @pl.when(pl.program_id(1) == 0)
    def _():
        state_ref[...] = jnp.zeros_like(state_ref)

    dh = state_ref.shape[1]
    for h in range(RET_HEADS):
        cols = slice(h * dh, (h + 1) * dh)
        for c in range(n_chunks):
            rows = slice(c * chunk, (c + 1) * chunk)
            q = q_ref[rows, cols]
            k = k_ref[rows, cols]
            v = v_ref[rows, cols]
            state = state_ref[h]
            scores = lax.dot_general(q, k, (((1,), (1,)), ((), ())), preferred_element_type=F32) * decay_ref[h]
            intra = jnp.dot(scores.astype(BF16), v, preferred_element_type=F32)
            inter = jnp.dot(q, state.astype(BF16), preferred_element_type=F32) * xi_ref[h]
            kz = (k.astype(F32) * zeta_ref[h]).astype(BF16)
            kv = lax.dot_general(kz, v, (((0,), (0,)), ((), ())), preferred_element_type=F32)
            state_ref[h] = state * cd_ref[h] + kv
            out = intra + inter
            out = out * lax.rsqrt(jnp.mean(out * out, axis=-1, keepdims=True) + EPS)
            o_ref[rows, cols] = (out * g_ref[rows, cols].astype(F32)).astype(o_ref.dtype)


def _retention(proj, width, decay, xi, zeta, cd, batch, seq, n_chunks=4):
    m = proj.shape[0]
    dh = width // RET_HEADS
    chunk = decay.shape[1]
    rows = n_chunks * chunk
    steps = seq // rows
    tok = lambda part: pl.BlockSpec((rows, width), lambda b, c: (b * steps + c, part))
    table = lambda t: pl.BlockSpec(t.shape, lambda b, c: (0, 0, 0))
    return pl.pallas_call(
        functools.partial(_retention_kernel, n_chunks=n_chunks, chunk=chunk),
        grid=(batch, steps),
        in_specs=[tok(0), tok(1), tok(2), tok(3), table(decay), table(xi), table(zeta), table(cd)],
        out_specs=tok(0),
        out_shape=jax.ShapeDtypeStruct((m, width), BF16),
        scratch_shapes=[pltpu.VMEM((RET_HEADS, dh, dh), F32)],
        compiler_params=_params("parallel", "arbitrary"),
        name="retention",
    )(proj, proj, proj, proj, decay, xi, zeta, cd)


def _sgu_kernel(u_ref, vg_ref, w_ref, b_ref, o_ref, *, n_chunks):
    row = lax.broadcasted_iota(jnp.int32, (CHUNK, CHUNK), 0)
    col = lax.broadcasted_iota(jnp.int32, (CHUNK, CHUNK), 1)
    causal = (row >= col).astype(F32)
    dg = b_ref.shape[2]
    for g in range(SGU_GROUPS):
        cols = slice(g * dg, (g + 1) * dg)
        w = (w_ref[g] * causal).astype(BF16)
        bias = b_ref[g]
        for c in range(n_chunks):
            rows = slice(c * CHUNK, (c + 1) * CHUNK)
            mixed = jnp.dot(w, vg_ref[rows, cols], preferred_element_type=F32) + bias
            o_ref[rows, cols] = (u_ref[rows, cols].astype(F32) * mixed).astype(o_ref.dtype)


def _sgu(proj, col0, width, w_s, bias, n_chunks=8):
    m = proj.shape[0]
    rows = n_chunks * CHUNK
    return pl.pallas_call(
        functools.partial(_sgu_kernel, n_chunks=n_chunks),
        grid=(m // rows,),
        in_specs=[pl.BlockSpec((rows, width), lambda i: (i, col0 // width)),
                  pl.BlockSpec((rows, width), lambda i: (i, col0 // width + 1)),
                  pl.BlockSpec(w_s.shape, lambda i: (0, 0, 0)),
                  pl.BlockSpec(bias.shape, lambda i: (0, 0, 0))],
        out_specs=pl.BlockSpec((rows, width), lambda i: (i, 0)),
        out_shape=jax.ShapeDtypeStruct((m, width), BF16),
        compiler_params=_params("parallel"),
        name="sgu",
    )(proj, proj, w_s, bias)


def _retention_tables(dh, chunk):
    log_gamma = jnp.log(1.0 - 2.0 ** (-5.0 - jnp.arange(RET_HEADS, dtype=F32)))
    idx = jnp.arange(chunk)
    dist = (idx[:, None] - idx[None, :]).astype(F32)
    decay = jnp.where(dist[None] >= 0, jnp.exp(log_gamma[:, None, None] * jnp.maximum(dist, 0.0)[None]), 0.0)
    xi = jnp.exp(log_gamma[:, None] * (idx + 1).astype(F32))
    zeta = jnp.exp(log_gamma[:, None] * (chunk - 1 - idx).astype(F32))
    cd = jnp.exp(log_gamma * chunk)
    bcast = lambda t: jnp.broadcast_to(t[:, :, None], (RET_HEADS, chunk, dh))
    return decay, bcast(xi), bcast(zeta), jnp.broadcast_to(cd[:, None, None], (RET_HEADS, 1, dh))


def kernel(x, c, positions, ada_w, ada_b, norm_ffn1_g, ffn1_w1, ffn1_w3, ffn1_w2, norm_mix_g, w_in, sgu_norm_g, sgu_norm_b, sgu_w_s, sgu_b_s, w_out, norm_ffn2_g, ffn2_w1, ffn2_w3, ffn2_w2, final_norm_g):
    batch, seq, d = x.shape
    m = batch * seq
    depth = ada_w.shape[0]
    d_ff = ffn1_w2.shape[1]
    ret_width = w_out.shape[1] // 2
    sgu_width = w_out.shape[1] - ret_width
    dh = ret_width // RET_HEADS
    dg = sgu_width // SGU_GROUPS

    x2 = x.reshape(m, d)
    c_pad = jnp.zeros((8, d), F32).at[:batch].set(c)
    inv_freq = ROPE_BASE ** (-jnp.arange(0, dh // 2, dtype=F32) / (dh // 2))
    cos, sin = _rope_tables(positions.reshape(m, 1), inv_freq)
    decay, xi, zeta, cd = _retention_tables(dh, RET_CHUNK)

    for l in range(depth):
        mod3 = _ada_mod(c_pad, ada_w[l], ada_b[l])[:batch].reshape(batch, 1, N_MOD * d)

        h = _norm_mod(x2, norm_ffn1_g[l], mod3, 0, 1, seq)
        act, (w2_b, w_in_b, w_out_b, w1_b, w3_b) = _ffn_up(
            h, ffn1_w1[l], ffn1_w3[l],
            [(ffn1_w2[l], 32), (w_in[l], 16), (w_out[l], 16), (ffn2_w1[l], 16), (ffn2_w3[l], 16)], n_tiles=1)
        x2, xb = _matmul_residual([act], [(w2_b, 0, d_ff)], x2, mod3, 2, 0.5, seq, 512, 512, (F32, BF16))

        h = _norm_mod(xb, norm_mix_g[l], mod3, 3, 4, seq)
        proj = _proj(h, w_in_b, cos, sin, sgu_norm_g[l].reshape(1, sgu_width), sgu_norm_b[l].reshape(1, sgu_width),
                     ret_width, sgu_width, dh, dg)
        ret = _retention(proj, ret_width, decay, xi, zeta, cd, batch, seq)
        bias = jnp.broadcast_to(sgu_b_s[l][:, :, None], (SGU_GROUPS, CHUNK, dg))
        gated = _sgu(proj, 4 * ret_width, sgu_width, sgu_w_s[l], bias)
        x2, xb = _matmul_residual([ret, gated], [(w_out_b, 0, ret_width), (w_out_b, 1, sgu_width)], x2, mod3, 5, 1.0,
                                  seq, 1024, 1024, (F32, BF16))

        h = _norm_mod(xb, norm_ffn2_g[l], mod3, 6, 7, seq)
        act, (w2_b,) = _ffn_up(h, w1_b, w3_b, [(ffn2_w2[l], 128)], n_tiles=2)
        last = l == depth - 1
        outs = _matmul_residual([act], [(w2_b, 0, d_ff)], x2, mod3, 8, 0.5, seq, 512, 512,
                                (BF16,) if last else (F32, BF16))
        x2, xb = (None, outs[0]) if last else outs

    return _final_norm(xb, final_norm_g).reshape(batch, seq, d)
```

```python
import functools

import jax
import jax.numpy as jnp
from jax import lax
from jax.experimental import pallas as pl
from jax.experimental.pallas import tpu as pltpu

F32 = jnp.float32
BF16 = jnp.bfloat16

RET_HEADS = 8
SGU_GROUPS = 8
CHUNK = 128
RET_CHUNK = 256
NORM_SLAB = 8
NORM_UNROLL = 8
PROJ_SUB_TILES = 2
ROPE_BASE = 10000.0
EPS = 1e-6
N_MOD = 9

V7X_VMEM_BYTES = 64 * 1024 * 1024
VMEM_LIMIT_BYTES = V7X_VMEM_BYTES - 3 * 1024 * 1024


def _params(*semantics):
    return pltpu.CompilerParams(dimension_semantics=semantics, vmem_limit_bytes=VMEM_LIMIT_BYTES)


def _silu(x):
    half = 0.5 * x
    return half + half * jnp.tanh(half)


def _gelu_tanh(x):
    return 0.5 * x * (1.0 + jnp.tanh(0.7978845608028654 * (x + 0.044715 * (x * x * x))))


def _ada_kernel(c_ref, w_ref, b_ref, o_ref):
    sc = _silu(c_ref[...]).astype(BF16)
    o_ref[...] = jnp.dot(sc, w_ref[...].astype(BF16), preferred_element_type=F32) + b_ref[...]


def _ada_rope_kernel(c_ref, w_ref, b_ref, pos_ref, freq_ref, o_ref, cos_ref, sin_ref):
    _ada_kernel(c_ref, w_ref, b_ref, o_ref)
    ang = pos_ref[...].astype(F32) * freq_ref[...]
    cos_ref[...] = jnp.cos(ang)
    sin_ref[...] = jnp.sin(ang)


def _ada_mod(c_pad, w, b, rope=None, tn=512, rope_rows=256):
    rows, d = c_pad.shape
    n = w.shape[1]
    in_specs = [
        pl.BlockSpec((rows, d), lambda j: (0, 0)),
        pl.BlockSpec((d, tn), lambda j: (0, j)),
        pl.BlockSpec((1, tn), lambda j: (0, j)),
    ]
    operands = [c_pad, w, b.reshape(1, n)]
    out_specs = [pl.BlockSpec((rows, tn), lambda j: (0, j))]
    out_shape = [jax.ShapeDtypeStruct((rows, n), F32)]
    if rope is not None:
        pos_col, inv_freq = rope
        tokens, half = pos_col.shape[0], inv_freq.shape[0]
        nb = tokens // rope_rows
        assert nb * rope_rows == tokens and nb <= n // tn
        walk = lambda j: (jnp.minimum(j, nb - 1), 0)
        in_specs += [pl.BlockSpec((rope_rows, 1), walk), pl.BlockSpec((1, half), lambda j: (0, 0))]
        operands += [pos_col, inv_freq.reshape(1, half)]
        out_specs += [pl.BlockSpec((rope_rows, half), walk)] * 2
        out_shape += [jax.ShapeDtypeStruct((tokens, half), F32)] * 2
    return pl.pallas_call(
        _ada_kernel if rope is None else _ada_rope_kernel,
        grid=(n // tn,),
        in_specs=in_specs,
        out_specs=out_specs,
        out_shape=out_shape,
        compiler_params=_params("arbitrary"),
        name="ada_mod",
    )(*operands)


def _norm_rows(x_ref, o_ref, finish, slab):
    def body(s, carry):
        rows = pl.ds(pl.multiple_of(s * slab, slab), slab)
        x = x_ref[rows, :].astype(F32)
        r = lax.rsqrt(jnp.mean(x * x, axis=-1, keepdims=True) + EPS)
        o_ref[rows, :] = finish(x * r).astype(o_ref.dtype)
        return carry

    lax.fori_loop(0, x_ref.shape[0] // slab, body, 0, unroll=NORM_UNROLL)


def _norm_mod_kernel(x_ref, g_ref, sh_ref, sc_ref, o_ref, *, slab):
    gain = g_ref[...] * (1.0 + sc_ref[0])
    shift = sh_ref[0]
    _norm_rows(x_ref, o_ref, lambda y: y * gain + shift, slab)


def _norm_kernel(x_ref, g_ref, o_ref, *, slab):
    g = g_ref[...]
    _norm_rows(x_ref, o_ref, lambda y: y * g, slab)


def _norm_mod(x2, g, mod3, seg_shift, seg_scale, seq, tm=1024):
    m, d = x2.shape
    bidx = lambda i: (i * tm) // seq
    return pl.pallas_call(
        functools.partial(_norm_mod_kernel, slab=NORM_SLAB),
        grid=(m // tm,),
        in_specs=[
            pl.BlockSpec((tm, d), lambda i: (i, 0)),
            pl.BlockSpec((1, d), lambda i: (0, 0)),
            pl.BlockSpec((1, 1, d), lambda i: (bidx(i), 0, seg_shift)),
            pl.BlockSpec((1, 1, d), lambda i: (bidx(i), 0, seg_scale)),
        ],
        out_specs=pl.BlockSpec((tm, d), lambda i: (i, 0)),
        out_shape=jax.ShapeDtypeStruct((m, d), BF16),
        compiler_params=_params("parallel"),
        name="norm_mod",
    )(x2, g.reshape(1, d), mod3, mod3)


def _final_norm(x2, g, tm=1024):
    m, d = x2.shape
    return pl.pallas_call(
        functools.partial(_norm_kernel, slab=NORM_SLAB),
        grid=(m // tm,),
        in_specs=[pl.BlockSpec((tm, d), lambda i: (i, 0)), pl.BlockSpec((1, d), lambda i: (0, 0))],
        out_specs=pl.BlockSpec((tm, d), lambda i: (i, 0)),
        out_shape=jax.ShapeDtypeStruct((m, d), F32),
        compiler_params=_params("parallel"),
        name="final_norm",
    )(x2, g.reshape(1, d))


def _as_bf16(ref):
    w = ref[...]
    return w if w.dtype == BF16 else w.astype(BF16)


def _ffn_up_kernel(*refs, n_tiles, n_side, last_tile):
    h_ref = refs[0]
    w1_refs = refs[1:1 + n_tiles]
    w3_refs = refs[1 + n_tiles:1 + 2 * n_tiles]
    side_in = refs[1 + 2 * n_tiles:1 + 2 * n_tiles + n_side]
    o_ref = refs[1 + 2 * n_tiles + n_side]
    side_out = refs[2 + 2 * n_tiles + n_side:]
    tn = w1_refs[0].shape[1]

    def tile(t):
        h = h_ref[...]
        a = jnp.dot(h, _as_bf16(w1_refs[t]), preferred_element_type=F32)
        b = jnp.dot(h, _as_bf16(w3_refs[t]), preferred_element_type=F32)
        o_ref[:, t * tn:(t + 1) * tn] = (_silu(a) * b).astype(o_ref.dtype)

    if n_tiles == 1:
        tile(0)
    else:
        first = pl.program_id(1) * n_tiles

        @pl.when(first + n_tiles - 1 <= last_tile)
        def _():
            for t in range(n_tiles):
                tile(t)

        @pl.when(first + n_tiles - 1 > last_tile)
        def _():
            for t in range(n_tiles):
                if_valid = first + t <= last_tile
                pl.when(if_valid)(functools.partial(tile, t))

                @pl.when(jnp.logical_not(if_valid))
                def _():
                    o_ref[:, t * tn:(t + 1) * tn] = jnp.zeros((o_ref.shape[0], tn), o_ref.dtype)

    for src, dst in zip(side_in, side_out):
        dst[...] = src[...].astype(dst.dtype)


def _ffn_up(h, w1, w3, side, n_tiles, tm=2048, tn=256):
    m, k = h.shape
    f = w1.shape[1]
    last_tile = f // tn - 1
    steps_j = pl.cdiv(last_tile + 1, n_tiles)
    walk = lambda nb: (lambda i, j: (jnp.minimum(i * steps_j + j, nb - 1), 0))
    wspecs = [pl.BlockSpec((k, tn), lambda i, j, t=t: (0, jnp.minimum(j * n_tiles + t, last_tile)))
              for t in range(n_tiles)]
    side_specs, side_shapes = [], []
    for w, rb in side:
        nb = w.shape[0] // rb
        assert nb * rb == w.shape[0] and nb <= (m // tm) * steps_j
        side_specs.append(pl.BlockSpec((rb, w.shape[1]), walk(nb)))
        side_shapes.append(jax.ShapeDtypeStruct(w.shape, BF16))
    outs = pl.pallas_call(
        functools.partial(_ffn_up_kernel, n_tiles=n_tiles, n_side=len(side), last_tile=last_tile),
        grid=(m // tm, steps_j),
        in_specs=[pl.BlockSpec((tm, k), lambda i, j: (i, 0), pipeline_mode=pl.Buffered(1))]
        + wspecs + wspecs + side_specs,
        out_specs=[pl.BlockSpec((tm, n_tiles * tn), lambda i, j: (i, j))] + side_specs,
        out_shape=[jax.ShapeDtypeStruct((m, steps_j * n_tiles * tn), BF16)] + side_shapes,
        compiler_params=_params("arbitrary", "arbitrary"),
        name="ffn_up",
    )(h, *([w1] * n_tiles), *([w3] * n_tiles), *[w for w, _ in side])
    return outs[0], outs[1:]


def _residual_kernel(*refs, n_pairs, scale):
    a_refs = refs[:n_pairs]
    b_refs = refs[n_pairs:2 * n_pairs]
    x_ref, gate_ref = refs[2 * n_pairs:2 * n_pairs + 2]
    o_refs = refs[2 * n_pairs + 2:]
    acc = None
    for a_ref, b_ref in zip(a_refs, b_refs):
        part = jnp.dot(a_ref[:, :b_ref.shape[0]], b_ref[...], preferred_element_type=F32)
        acc = part if acc is None else acc + part
    y = x_ref[...] + (scale * gate_ref[0]) * acc
    for o_ref in o_refs:
        o_ref[...] = y.astype(o_ref.dtype)


def _matmul_residual(a_list, b_list, x2, mod3, seg_gate, scale, seq, tm, tn, out_dtypes):
    m, n = x2.shape
    n_pairs = len(a_list)
    in_specs = []
    for a in a_list:
        in_specs.append(pl.BlockSpec((tm, a.shape[1]), lambda j, i: (i, 0)))
    operands = list(a_list)
    for w, row_block, rows in b_list:
        in_specs.append(pl.BlockSpec((rows, tn), lambda j, i, rb=row_block: (rb, j)))
        operands.append(w)
    in_specs.append(pl.BlockSpec((tm, tn), lambda j, i: (i, j)))
    in_specs.append(pl.BlockSpec((1, 1, tn), lambda j, i: ((i * tm) // seq, 0, seg_gate * (n // tn) + j)))
    return pl.pallas_call(
        functools.partial(_residual_kernel, n_pairs=n_pairs, scale=scale),
        grid=(n // tn, m // tm),
        in_specs=in_specs,
        out_specs=[pl.BlockSpec((tm, tn), lambda j, i: (i, j))] * len(out_dtypes),
        out_shape=[jax.ShapeDtypeStruct((m, n), dt) for dt in out_dtypes],
        compiler_params=_params("parallel", "parallel"),
        name="matmul_residual",
    )(*operands, x2, mod3)


def _proj_kernel(h_ref, w_ref, cos_ref, sin_ref, gn_g_ref, gn_b_ref, o_ref, *, ends, head_dim, group_dim, k_scale):
    j = pl.program_id(1)
    q_end, k_end, v_end, g_end, u_end = ends
    tn = w_ref.shape[1]

    def run(epilogue):
        sub = h_ref.shape[0] // PROJ_SUB_TILES
        for r in range(PROJ_SUB_TILES):
            rows = slice(r * sub, (r + 1) * sub)
            acc = jnp.dot(h_ref[rows, :], w_ref[...], preferred_element_type=F32)
            epilogue(acc, rows)

    def rotary(acc, rows):
        scale = jnp.where(j >= q_end, k_scale, 1.0).astype(F32)
        cos = cos_ref[rows, :] * scale
        sin = sin_ref[rows, :] * scale
        half = head_dim // 2
        for hd in range(tn // head_dim):
            lo = slice(hd * head_dim, hd * head_dim + half)
            hi = slice(hd * head_dim + half, (hd + 1) * head_dim)
            t1, t2 = acc[:, lo], acc[:, hi]
            o_ref[rows, lo] = (t1 * cos - t2 * sin).astype(o_ref.dtype)
            o_ref[rows, hi] = (t1 * sin + t2 * cos).astype(o_ref.dtype)

    def pointwise(act):
        def epilogue(acc, rows):
            o_ref[rows, :] = act(acc).astype(o_ref.dtype)
        return epilogue

    def gelu_groupnorm(acc, rows):
        for gi in range(tn // group_dim):
            cols = slice(gi * group_dim, (gi + 1) * group_dim)
            v = _gelu_tanh(acc[:, cols])
            mu = jnp.mean(v, axis=-1, keepdims=True)
            cen = v - mu
            var = jnp.mean(cen * cen, axis=-1, keepdims=True)
            o_ref[rows, cols] = (cen * lax.rsqrt(var + EPS) * gn_g_ref[:, cols] + gn_b_ref[:, cols]).astype(
                o_ref.dtype)

    pl.when(j < k_end)(lambda: run(rotary))
    pl.when((j >= k_end) & (j < v_end))(lambda: run(pointwise(lambda t: t)))
    pl.when((j >= v_end) & (j < g_end))(lambda: run(pointwise(_silu)))
    pl.when((j >= g_end) & (j < u_end))(lambda: run(pointwise(_gelu_tanh)))
    pl.when(j >= u_end)(lambda: run(gelu_groupnorm))


def _proj(h, w, cos, sin, gn_g, gn_b, ret_width, sgu_width, head_dim, group_dim, tm=1024, tn=1024):
    m, k = h.shape
    n = w.shape[1]
    rb, sb = ret_width // tn, sgu_width // tn
    ends = (rb, 2 * rb, 3 * rb, 4 * rb, 4 * rb + sb)
    gn_spec = pl.BlockSpec((1, tn), lambda i, j: (0, jnp.clip(j - ends[4], 0, sb - 1)))
    rope_spec = pl.BlockSpec((tm, head_dim // 2), lambda i, j: (i, 0))
    return pl.pallas_call(
        functools.partial(_proj_kernel, ends=ends, head_dim=head_dim, group_dim=group_dim,
                          k_scale=head_dim ** -0.5),
        grid=(m // tm, n // tn),
        in_specs=[pl.BlockSpec((tm, k), lambda i, j: (i, 0)), pl.BlockSpec((k, tn), lambda i, j: (0, j)),
                  rope_spec, rope_spec, gn_spec, gn_spec],
        out_specs=pl.BlockSpec((tm, tn), lambda i, j: (i, j)),
        out_shape=jax.ShapeDtypeStruct((m, n), BF16),
        compiler_params=_params("parallel", "arbitrary"),
        name="proj",
    )(h, w, cos, sin, gn_g, gn_b)


def _retention_kernel(q_ref, k_ref, v_ref, g_ref, decay_ref, xi_ref, zeta_ref, cd_ref, o_ref, state_ref,
                      *, n_chunks, chunk):
    @pl.when(pl.program_id(1) == 0)
    def _():
        state_ref[...] = jnp.zeros_like(state_ref)

    dh = state_ref.shape[1]
    for h in range(RET_HEADS):
        cols = slice(h * dh, (h + 1) * dh)
        for c in range(n_chunks):
            rows = slice(c * chunk, (c + 1) * chunk)
            q = q_ref[rows, cols]
            k = k_ref[rows, cols]
            v = v_ref[rows, cols]
            state = state_ref[h]
            scores = lax.dot_general(q, k, (((1,), (1,)), ((), ())), preferred_element_type=F32) * decay_ref[h]
            intra = jnp.dot(scores.astype(BF16), v, preferred_element_type=F32)
            inter = jnp.dot(q, state.astype(BF16), preferred_element_type=F32) * xi_ref[h]
            kz = (k.astype(F32) * zeta_ref[h]).astype(BF16)
            kv = lax.dot_general(kz, v, (((0,), (0,)), ((), ())), preferred_element_type=F32)
            state_ref[h] = state * cd_ref[h] + kv
            out = intra + inter
            out = out * lax.rsqrt(jnp.mean(out * out, axis=-1, keepdims=True) + EPS)
            o_ref[rows, cols] = (out * g_ref[rows, cols].astype(F32)).astype(o_ref.dtype)


def _retention(proj, width, decay, xi, zeta, cd, batch, seq, n_chunks=4):
    m = proj.shape[0]
    dh = width // RET_HEADS
    chunk = decay.shape[1]
    rows = n_chunks * chunk
    steps = seq // rows
    tok = lambda part: pl.BlockSpec((rows, width), lambda b, c: (b * steps + c, part))
    table = lambda t: pl.BlockSpec(t.shape, lambda b, c: (0, 0, 0))
    return pl.pallas_call(
        functools.partial(_retention_kernel, n_chunks=n_chunks, chunk=chunk),
        grid=(batch, steps),
        in_specs=[tok(0), tok(1), tok(2), tok(3), table(decay), table(xi), table(zeta), table(cd)],
        out_specs=tok(0),
        out_shape=jax.ShapeDtypeStruct((m, width), BF16),
        scratch_shapes=[pltpu.VMEM((RET_HEADS, dh, dh), F32)],
        compiler_params=_params("parallel", "arbitrary"),
        name="retention",
    )(proj, proj, proj, proj, decay, xi, zeta, cd)


def _sgu_kernel(u_ref, vg_ref, w_ref, b_ref, o_ref, *, n_chunks):
    row = lax.broadcasted_iota(jnp.int32, (CHUNK, CHUNK), 0)
    col = lax.broadcasted_iota(jnp.int32, (CHUNK, CHUNK), 1)
    causal = (row >= col).astype(F32)
    dg = b_ref.shape[2]
    for g in range(SGU_GROUPS):
        cols = slice(g * dg, (g + 1) * dg)
        w = (w_ref[g] * causal).astype(BF16)
        bias = b_ref[g]
        for c in range(n_chunks):
            rows = slice(c * CHUNK, (c + 1) * CHUNK)
            mixed = jnp.dot(w, vg_ref[rows, cols], preferred_element_type=F32) + bias
            o_ref[rows, cols] = (u_ref[rows, cols].astype(F32) * mixed).astype(o_ref.dtype)


def _sgu(proj, col0, width, w_s, bias, n_chunks=8):
    m = proj.shape[0]
    rows = n_chunks * CHUNK
    return pl.pallas_call(
        functools.partial(_sgu_kernel, n_chunks=n_chunks),
        grid=(m // rows,),
        in_specs=[pl.BlockSpec((rows, width), lambda i: (i, col0 // width)),
                  pl.BlockSpec((rows, width), lambda i: (i, col0 // width + 1)),
                  pl.BlockSpec(w_s.shape, lambda i: (0, 0, 0)),
                  pl.BlockSpec(bias.shape, lambda i: (0, 0, 0))],
        out_specs=pl.BlockSpec((rows, width), lambda i: (i, 0)),
        out_shape=jax.ShapeDtypeStruct((m, width), BF16),
        compiler_params=_params("parallel"),
        name="sgu",
    )(proj, proj, w_s, bias)


def _retention_tables(dh, chunk):
    log_gamma = jnp.log(1.0 - 2.0 ** (-5.0 - jnp.arange(RET_HEADS, dtype=F32)))
    idx = jnp.arange(chunk)
    dist = (idx[:, None] - idx[None, :]).astype(F32)
    decay = jnp.where(dist[None] >= 0, jnp.exp(log_gamma[:, None, None] * jnp.maximum(dist, 0.0)[None]), 0.0)
    xi = jnp.exp(log_gamma[:, None] * (idx + 1).astype(F32))
    zeta = jnp.exp(log_gamma[:, None] * (chunk - 1 - idx).astype(F32))
    cd = jnp.exp(log_gamma * chunk)
    bcast = lambda t: jnp.broadcast_to(t[:, :, None], (RET_HEADS, chunk, dh))
    return decay, bcast(xi), bcast(zeta), jnp.broadcast_to(cd[:, None, None], (RET_HEADS, 1, dh))


def kernel(x, c, positions, ada_w, ada_b, norm_ffn1_g, ffn1_w1, ffn1_w3, ffn1_w2, norm_mix_g, w_in, sgu_norm_g, sgu_norm_b, sgu_w_s, sgu_b_s, w_out, norm_ffn2_g, ffn2_w1, ffn2_w3, ffn2_w2, final_norm_g):
    batch, seq, d = x.shape
    m = batch * seq
    depth = ada_w.shape[0]
    d_ff = ffn1_w2.shape[1]
    ret_width = w_out.shape[1] // 2
    sgu_width = w_out.shape[1] - ret_width
    dh = ret_width // RET_HEADS
    dg = sgu_width // SGU_GROUPS

    x2 = x.reshape(m, d)
    c_pad = jnp.zeros((8, d), F32).at[:batch].set(c)
    inv_freq = ROPE_BASE ** (-jnp.arange(0, dh // 2, dtype=F32) / (dh // 2))
    decay, xi, zeta, cd = _retention_tables(dh, RET_CHUNK)

    for l in range(depth):
        if l == 0:
            mod, cos, sin = _ada_mod(c_pad, ada_w[l], ada_b[l], rope=(positions.reshape(m, 1), inv_freq))
        else:
            mod, = _ada_mod(c_pad, ada_w[l], ada_b[l])
        mod3 = mod[:batch].reshape(batch, 1, N_MOD * d)

        h = _norm_mod(x2, norm_ffn1_g[l], mod3, 0, 1, seq)
        act, (w2_b, w_in_b, w_out_b, w1_b, w3_b) = _ffn_up(
            h, ffn1_w1[l], ffn1_w3[l],
            [(ffn1_w2[l], 32), (w_in[l], 16), (w_out[l], 16), (ffn2_w1[l], 16), (ffn2_w3[l], 16)], n_tiles=1)
        x2, xb = _matmul_residual([act], [(w2_b, 0, d_ff)], x2, mod3, 2, 0.5, seq, 512, 512, (F32, BF16))

        h = _norm_mod(xb, norm_mix_g[l], mod3, 3, 4, seq)
        proj = _proj(h, w_in_b, cos, sin, sgu_norm_g[l].reshape(1, sgu_width), sgu_norm_b[l].reshape(1, sgu_width),
                     ret_width, sgu_width, dh, dg)
        ret = _retention(proj, ret_width, decay, xi, zeta, cd, batch, seq)
        bias = jnp.broadcast_to(sgu_b_s[l][:, :, None], (SGU_GROUPS, CHUNK, dg))
        gated = _sgu(proj, 4 * ret_width, sgu_width, sgu_w_s[l], bias)
        x2, xb = _matmul_residual([ret, gated], [(w_out_b, 0, ret_width), (w_out_b, 1, sgu_width)], x2, mod3, 5, 1.0,
                                  seq, 1024, 1024, (F32, BF16))

        h = _norm_mod(xb, norm_ffn2_g[l], mod3, 6, 7, seq)
        act, (w2_b,) = _ffn_up(h, w1_b, w3_b, [(ffn2_w2[l], 128)], n_tiles=2)
        last = l == depth - 1
        outs = _matmul_residual([act], [(w2_b, 0, d_ff)], x2, mod3, 8, 0.5, seq, 512, 512,
                                (BF16,) if last else (F32, BF16))
        x2, xb = (None, outs[0]) if last else outs

    return _final_norm(xb, final_norm_g).reshape(batch, seq, d)
```

```python
import functools

import jax
import jax.numpy as jnp
from jax import lax
from jax.experimental import pallas as pl
from jax.experimental.pallas import tpu as pltpu

F32 = jnp.float32
BF16 = jnp.bfloat16

RET_HEADS = 8
SGU_GROUPS = 8
CHUNK = 128
RET_CHUNK = 256
NORM_SLAB = 8
NORM_UNROLL = 8
PROJ_SUB_TILES = 2
ROPE_BASE = 10000.0
GELU_C = 0.7978845608028654
GELU_A = 0.044715
EPS = 1e-6
N_MOD = 9

V7X_VMEM_BYTES = 64 * 1024 * 1024
VMEM_LIMIT_BYTES = V7X_VMEM_BYTES - 3 * 1024 * 1024


def _params(*semantics):
    return pltpu.CompilerParams(dimension_semantics=semantics, vmem_limit_bytes=VMEM_LIMIT_BYTES)


def _silu(x):
    half = 0.5 * x
    return half + half * jnp.tanh(half)


def _gelu_tanh(x):
    half = 0.5 * x
    return half + half * jnp.tanh(x * (GELU_C + (GELU_C * GELU_A) * (x * x)))


def _ada_kernel(c_ref, w_ref, b_ref, o_ref):
    sc = _silu(c_ref[...]).astype(BF16)
    o_ref[...] = jnp.dot(sc, w_ref[...].astype(BF16), preferred_element_type=F32) + b_ref[...]


def _ada_rope_kernel(c_ref, w_ref, b_ref, pos_ref, freq_ref, o_ref, cos_ref, sin_ref):
    _ada_kernel(c_ref, w_ref, b_ref, o_ref)
    ang = pos_ref[...].astype(F32) * freq_ref[...]
    cos_ref[...] = jnp.cos(ang)
    sin_ref[...] = jnp.sin(ang)


def _ada_mod(c_pad, w, b, rope=None, tn=512, rope_rows=256):
    rows, d = c_pad.shape
    n = w.shape[1]
    in_specs = [
        pl.BlockSpec((rows, d), lambda j: (0, 0)),
        pl.BlockSpec((d, tn), lambda j: (0, j)),
        pl.BlockSpec((1, tn), lambda j: (0, j)),
    ]
    operands = [c_pad, w, b.reshape(1, n)]
    out_specs = [pl.BlockSpec((rows, tn), lambda j: (0, j))]
    out_shape = [jax.ShapeDtypeStruct((rows, n), F32)]
    if rope is not None:
        pos_col, inv_freq = rope
        tokens, half = pos_col.shape[0], inv_freq.shape[0]
        nb = tokens // rope_rows
        assert nb * rope_rows == tokens and nb <= n // tn
        walk = lambda j: (jnp.minimum(j, nb - 1), 0)
        in_specs += [pl.BlockSpec((rope_rows, 1), walk), pl.BlockSpec((1, half), lambda j: (0, 0))]
        operands += [pos_col, inv_freq.reshape(1, half)]
        out_specs += [pl.BlockSpec((rope_rows, half), walk)] * 2
        out_shape += [jax.ShapeDtypeStruct((tokens, half), F32)] * 2
    return pl.pallas_call(
        _ada_kernel if rope is None else _ada_rope_kernel,
        grid=(n // tn,),
        in_specs=in_specs,
        out_specs=out_specs,
        out_shape=out_shape,
        compiler_params=_params("arbitrary"),
        name="ada_mod",
    )(*operands)


def _norm_rows(x_ref, o_ref, finish, slab):
    def body(s, carry):
        rows = pl.ds(pl.multiple_of(s * slab, slab), slab)
        x = x_ref[rows, :].astype(F32)
        r = lax.rsqrt(jnp.mean(x * x, axis=-1, keepdims=True) + EPS)
        o_ref[rows, :] = finish(x * r).astype(o_ref.dtype)
        return carry

    lax.fori_loop(0, x_ref.shape[0] // slab, body, 0, unroll=NORM_UNROLL)


def _norm_mod_kernel(x_ref, g_ref, sh_ref, sc_ref, o_ref, *, slab):
    gain = g_ref[...] * (1.0 + sc_ref[0])
    shift = sh_ref[0]
    _norm_rows(x_ref, o_ref, lambda y: y * gain + shift, slab)


def _norm_kernel(x_ref, g_ref, o_ref, *, slab):
    g = g_ref[...]
    _norm_rows(x_ref, o_ref, lambda y: y * g, slab)


def _norm_mod(x2, g, mod3, seg_shift, seg_scale, seq, tm=1024):
    m, d = x2.shape
    bidx = lambda i: (i * tm) // seq
    return pl.pallas_call(
        functools.partial(_norm_mod_kernel, slab=NORM_SLAB),
        grid=(m // tm,),
        in_specs=[
            pl.BlockSpec((tm, d), lambda i: (i, 0)),
            pl.BlockSpec((1, d), lambda i: (0, 0)),
            pl.BlockSpec((1, 1, d), lambda i: (bidx(i), 0, seg_shift)),
            pl.BlockSpec((1, 1, d), lambda i: (bidx(i), 0, seg_scale)),
        ],
        out_specs=pl.BlockSpec((tm, d), lambda i: (i, 0)),
        out_shape=jax.ShapeDtypeStruct((m, d), BF16),
        compiler_params=_params("parallel"),
        name="norm_mod",
    )(x2, g.reshape(1, d), mod3, mod3)


def _final_norm(x2, g, tm=1024):
    m, d = x2.shape
    return pl.pallas_call(
        functools.partial(_norm_kernel, slab=NORM_SLAB),
        grid=(m // tm,),
        in_specs=[pl.BlockSpec((tm, d), lambda i: (i, 0)), pl.BlockSpec((1, d), lambda i: (0, 0))],
        out_specs=pl.BlockSpec((tm, d), lambda i: (i, 0)),
        out_shape=jax.ShapeDtypeStruct((m, d), F32),
        compiler_params=_params("parallel"),
        name="final_norm",
    )(x2, g.reshape(1, d))


def _as_bf16(ref):
    w = ref[...]
    return w if w.dtype == BF16 else w.astype(BF16)


def _ffn_up_kernel(*refs, n_tiles, n_side, last_tile):
    h_ref = refs[0]
    w1_refs = refs[1:1 + n_tiles]
    w3_refs = refs[1 + n_tiles:1 + 2 * n_tiles]
    side_in = refs[1 + 2 * n_tiles:1 + 2 * n_tiles + n_side]
    o_ref = refs[1 + 2 * n_tiles + n_side]
    side_out = refs[2 + 2 * n_tiles + n_side:]
    tn = w1_refs[0].shape[1]

    def tile(t):
        h = h_ref[...]
        a = jnp.dot(h, _as_bf16(w1_refs[t]), preferred_element_type=F32)
        b = jnp.dot(h, _as_bf16(w3_refs[t]), preferred_element_type=F32)
        o_ref[:, t * tn:(t + 1) * tn] = (_silu(a) * b).astype(o_ref.dtype)

    if n_tiles == 1:
        tile(0)
    else:
        first = pl.program_id(1) * n_tiles

        @pl.when(first + n_tiles - 1 <= last_tile)
        def _():
            for t in range(n_tiles):
                tile(t)

        @pl.when(first + n_tiles - 1 > last_tile)
        def _():
            for t in range(n_tiles):
                if_valid = first + t <= last_tile
                pl.when(if_valid)(functools.partial(tile, t))

                @pl.when(jnp.logical_not(if_valid))
                def _():
                    o_ref[:, t * tn:(t + 1) * tn] = jnp.zeros((o_ref.shape[0], tn), o_ref.dtype)

    for src, dst in zip(side_in, side_out):
        dst[...] = src[...].astype(dst.dtype)


def _ffn_up(h, w1, w3, side, n_tiles, tm=2048, tn=256):
    m, k = h.shape
    f = w1.shape[1]
    last_tile = f // tn - 1
    steps_j = pl.cdiv(last_tile + 1, n_tiles)
    walk = lambda nb: (lambda i, j: (jnp.minimum(i * steps_j + j, nb - 1), 0))
    wspecs = [pl.BlockSpec((k, tn), lambda i, j, t=t: (0, jnp.minimum(j * n_tiles + t, last_tile)))
              for t in range(n_tiles)]
    side_specs, side_shapes = [], []
    for w, rb in side:
        nb = w.shape[0] // rb
        assert nb * rb == w.shape[0] and nb <= (m // tm) * steps_j
        side_specs.append(pl.BlockSpec((rb, w.shape[1]), walk(nb)))
        side_shapes.append(jax.ShapeDtypeStruct(w.shape, BF16))
    outs = pl.pallas_call(
        functools.partial(_ffn_up_kernel, n_tiles=n_tiles, n_side=len(side), last_tile=last_tile),
        grid=(m // tm, steps_j),
        in_specs=[pl.BlockSpec((tm, k), lambda i, j: (i, 0), pipeline_mode=pl.Buffered(1))]
        + wspecs + wspecs + side_specs,
        out_specs=[pl.BlockSpec((tm, n_tiles * tn), lambda i, j: (i, j))] + side_specs,
        out_shape=[jax.ShapeDtypeStruct((m, steps_j * n_tiles * tn), BF16)] + side_shapes,
        compiler_params=_params("arbitrary", "arbitrary"),
        name="ffn_up",
    )(h, *([w1] * n_tiles), *([w3] * n_tiles), *[w for w, _ in side])
    return outs[0], outs[1:]


def _residual_kernel(*refs, n_pairs, scale):
    a_refs = refs[:n_pairs]
    b_refs = refs[n_pairs:2 * n_pairs]
    x_ref, gate_ref = refs[2 * n_pairs:2 * n_pairs + 2]
    o_refs = refs[2 * n_pairs + 2:]
    acc = None
    for a_ref, b_ref in zip(a_refs, b_refs):
        part = jnp.dot(a_ref[:, :b_ref.shape[0]], b_ref[...], preferred_element_type=F32)
        acc = part if acc is None else acc + part
    y = x_ref[...] + (scale * gate_ref[0]) * acc
    for o_ref in o_refs:
        o_ref[...] = y.astype(o_ref.dtype)


def _matmul_residual(a_list, b_list, x2, mod3, seg_gate, scale, seq, tm, tn, out_dtypes):
    m, n = x2.shape
    n_pairs = len(a_list)
    in_specs = []
    for a in a_list:
        in_specs.append(pl.BlockSpec((tm, a.shape[1]), lambda j, i: (i, 0)))
    operands = list(a_list)
    for w, row_block, rows in b_list:
        in_specs.append(pl.BlockSpec((rows, tn), lambda j, i, rb=row_block: (rb, j)))
        operands.append(w)
    in_specs.append(pl.BlockSpec((tm, tn), lambda j, i: (i, j)))
    in_specs.append(pl.BlockSpec((1, 1, tn), lambda j, i: ((i * tm) // seq, 0, seg_gate * (n // tn) + j)))
    return pl.pallas_call(
        functools.partial(_residual_kernel, n_pairs=n_pairs, scale=scale),
        grid=(n // tn, m // tm),
        in_specs=in_specs,
        out_specs=[pl.BlockSpec((tm, tn), lambda j, i: (i, j))] * len(out_dtypes),
        out_shape=[jax.ShapeDtypeStruct((m, n), dt) for dt in out_dtypes],
        compiler_params=_params("parallel", "parallel"),
        name="matmul_residual",
    )(*operands, x2, mod3)


def _proj_kernel(h_ref, w_ref, cos_ref, sin_ref, gn_g_ref, gn_b_ref, o_ref, *, ends, head_dim, group_dim, k_scale):
    j = pl.program_id(1)
    q_end, k_end, v_end, g_end, u_end = ends
    tn = w_ref.shape[1]

    def run(epilogue):
        sub = h_ref.shape[0] // PROJ_SUB_TILES
        for r in range(PROJ_SUB_TILES):
            rows = slice(r * sub, (r + 1) * sub)
            acc = jnp.dot(h_ref[rows, :], w_ref[...], preferred_element_type=F32)
            epilogue(acc, rows)

    def rotary(acc, rows):
        scale = jnp.where(j >= q_end, k_scale, 1.0).astype(F32)
        cos = cos_ref[rows, :] * scale
        sin = sin_ref[rows, :] * scale
        half = head_dim // 2
        for hd in range(tn // head_dim):
            lo = slice(hd * head_dim, hd * head_dim + half)
            hi = slice(hd * head_dim + half, (hd + 1) * head_dim)
            t1, t2 = acc[:, lo], acc[:, hi]
            o_ref[rows, lo] = (t1 * cos - t2 * sin).astype(o_ref.dtype)
            o_ref[rows, hi] = (t1 * sin + t2 * cos).astype(o_ref.dtype)

    def pointwise(act):
        def epilogue(acc, rows):
            o_ref[rows, :] = act(acc).astype(o_ref.dtype)
        return epilogue

    def gelu_groupnorm(acc, rows):
        for gi in range(tn // group_dim):
            cols = slice(gi * group_dim, (gi + 1) * group_dim)
            v = _gelu_tanh(acc[:, cols])
            mu = jnp.mean(v, axis=-1, keepdims=True)
            cen = v - mu
            var = jnp.mean(cen * cen, axis=-1, keepdims=True)
            o_ref[rows, cols] = (cen * lax.rsqrt(var + EPS) * gn_g_ref[:, cols] + gn_b_ref[:, cols]).astype(
                o_ref.dtype)

    pl.when(j < k_end)(lambda: run(rotary))
    plain = ((j >= k_end) & (j < v_end)) | ((j >= g_end) & (j < u_end))
    pl.when(plain)(lambda: run(pointwise(lambda t: t)))
    pl.when((j >= v_end) & (j < g_end))(lambda: run(pointwise(_silu)))
    pl.when(j >= u_end)(lambda: run(gelu_groupnorm))


def _proj(h, w, cos, sin, gn_g, gn_b, ret_width, sgu_width, head_dim, group_dim, tm=1024, tn=1024):
    m, k = h.shape
    n = w.shape[1]
    rb, sb = ret_width // tn, sgu_width // tn
    ends = (rb, 2 * rb, 3 * rb, 4 * rb, 4 * rb + sb)
    gn_spec = pl.BlockSpec((1, tn), lambda i, j: (0, jnp.clip(j - ends[4], 0, sb - 1)))
    rope_spec = pl.BlockSpec((tm, head_dim // 2), lambda i, j: (i, 0))
    return pl.pallas_call(
        functools.partial(_proj_kernel, ends=ends, head_dim=head_dim, group_dim=group_dim,
                          k_scale=head_dim ** -0.5),
        grid=(m // tm, n // tn),
        in_specs=[pl.BlockSpec((tm, k), lambda i, j: (i, 0)), pl.BlockSpec((k, tn), lambda i, j: (0, j)),
                  rope_spec, rope_spec, gn_spec, gn_spec],
        out_specs=pl.BlockSpec((tm, tn), lambda i, j: (i, j)),
        out_shape=jax.ShapeDtypeStruct((m, n), BF16),
        compiler_params=_params("parallel", "arbitrary"),
        name="proj",
    )(h, w, cos, sin, gn_g, gn_b)


def _retention_kernel(q_ref, k_ref, v_ref, g_ref, decay_ref, xi_ref, zeta_ref, cd_ref, o_ref, state_ref,
                      *, n_chunks, chunk):
    @pl.when(pl.program_id(1) == 0)
    def _():
        state_ref[...] = jnp.zeros_like(state_ref)

    dh = state_ref.shape[1]
    for h in range(RET_HEADS):
        cols = slice(h * dh, (h + 1) * dh)
        for c in range(n_chunks):
            rows = slice(c * chunk, (c + 1) * chunk)
            q = q_ref[rows, cols]
            k = k_ref[rows, cols]
            v = v_ref[rows, cols]
            state = state_ref[h]
            scores = lax.dot_general(q, k, (((1,), (1,)), ((), ())), preferred_element_type=F32) * decay_ref[h]
            intra = jnp.dot(scores.astype(BF16), v, preferred_element_type=F32)
            inter = jnp.dot(q, state.astype(BF16), preferred_element_type=F32) * xi_ref[h]
            kz = (k.astype(F32) * zeta_ref[h]).astype(BF16)
            kv = lax.dot_general(kz, v, (((0,), (0,)), ((), ())), preferred_element_type=F32)
            state_ref[h] = state * cd_ref[h] + kv
            out = intra + inter
            out = out * lax.rsqrt(jnp.mean(out * out, axis=-1, keepdims=True) + EPS)
            o_ref[rows, cols] = (out * g_ref[rows, cols].astype(F32)).astype(o_ref.dtype)


def _retention(proj, width, decay, xi, zeta, cd, batch, seq, n_chunks=4):
    m = proj.shape[0]
    dh = width // RET_HEADS
    chunk = decay.shape[1]
    rows = n_chunks * chunk
    steps = seq // rows
    tok = lambda part: pl.BlockSpec((rows, width), lambda b, c: (b * steps + c, part))
    table = lambda t: pl.BlockSpec(t.shape, lambda b, c: (0, 0, 0))
    return pl.pallas_call(
        functools.partial(_retention_kernel, n_chunks=n_chunks, chunk=chunk),
        grid=(batch, steps),
        in_specs=[tok(0), tok(1), tok(2), tok(3), table(decay), table(xi), table(zeta), table(cd)],
        out_specs=tok(0),
        out_shape=jax.ShapeDtypeStruct((m, width), BF16),
        scratch_shapes=[pltpu.VMEM((RET_HEADS, dh, dh), F32)],
        compiler_params=_params("parallel", "arbitrary"),
        name="retention",
    )(proj, proj, proj, proj, decay, xi, zeta, cd)


def _sgu_kernel(u_ref, vg_ref, w_ref, b_ref, o_ref, *, n_chunks):
    row = lax.broadcasted_iota(jnp.int32, (CHUNK, CHUNK), 0)
    col = lax.broadcasted_iota(jnp.int32, (CHUNK, CHUNK), 1)
    causal = (row >= col).astype(F32)
    dg = b_ref.shape[2]
    for g in range(SGU_GROUPS):
        cols = slice(g * dg, (g + 1) * dg)
        w = (w_ref[g] * causal).astype(BF16)
        bias = b_ref[g]
        for c in range(n_chunks):
            rows = slice(c * CHUNK, (c + 1) * CHUNK)
            mixed = jnp.dot(w, vg_ref[rows, cols], preferred_element_type=F32) + bias
            o_ref[rows, cols] = (_gelu_tanh(u_ref[rows, cols].astype(F32)) * mixed).astype(o_ref.dtype)


def _sgu(proj, col0, width, w_s, bias, n_chunks=8):
    m = proj.shape[0]
    rows = n_chunks * CHUNK
    return pl.pallas_call(
        functools.partial(_sgu_kernel, n_chunks=n_chunks),
        grid=(m // rows,),
        in_specs=[pl.BlockSpec((rows, width), lambda i: (i, col0 // width)),
                  pl.BlockSpec((rows, width), lambda i: (i, col0 // width + 1)),
                  pl.BlockSpec(w_s.shape, lambda i: (0, 0, 0)),
                  pl.BlockSpec(bias.shape, lambda i: (0, 0, 0))],
        out_specs=pl.BlockSpec((rows, width), lambda i: (i, 0)),
        out_shape=jax.ShapeDtypeStruct((m, width), BF16),
        compiler_params=_params("parallel"),
        name="sgu",
    )(proj, proj, w_s, bias)


def _retention_tables(dh, chunk):
    log_gamma = jnp.log(1.0 - 2.0 ** (-5.0 - jnp.arange(RET_HEADS, dtype=F32)))
    idx = jnp.arange(chunk)
    dist = (idx[:, None] - idx[None, :]).astype(F32)
    decay = jnp.where(dist[None] >= 0, jnp.exp(log_gamma[:, None, None] * jnp.maximum(dist, 0.0)[None]), 0.0)
    xi = jnp.exp(log_gamma[:, None] * (idx + 1).astype(F32))
    zeta = jnp.exp(log_gamma[:, None] * (chunk - 1 - idx).astype(F32))
    cd = jnp.exp(log_gamma * chunk)
    bcast = lambda t: jnp.broadcast_to(t[:, :, None], (RET_HEADS, chunk, dh))
    return decay, bcast(xi), bcast(zeta), jnp.broadcast_to(cd[:, None, None], (RET_HEADS, 1, dh))


def kernel(x, c, positions, ada_w, ada_b, norm_ffn1_g, ffn1_w1, ffn1_w3, ffn1_w2, norm_mix_g, w_in, sgu_norm_g, sgu_norm_b, sgu_w_s, sgu_b_s, w_out, norm_ffn2_g, ffn2_w1, ffn2_w3, ffn2_w2, final_norm_g):
    batch, seq, d = x.shape
    m = batch * seq
    depth = ada_w.shape[0]
    d_ff = ffn1_w2.shape[1]
    ret_width = w_out.shape[1] // 2
    sgu_width = w_out.shape[1] - ret_width
    dh = ret_width // RET_HEADS
    dg = sgu_width // SGU_GROUPS

    x2 = x.reshape(m, d)
    c_pad = jnp.zeros((8, d), F32).at[:batch].set(c)
    inv_freq = ROPE_BASE ** (-jnp.arange(0, dh // 2, dtype=F32) / (dh // 2))
    decay, xi, zeta, cd = _retention_tables(dh, RET_CHUNK)

    for l in range(depth):
        if l == 0:
            mod, cos, sin = _ada_mod(c_pad, ada_w[l], ada_b[l], rope=(positions.reshape(m, 1), inv_freq))
        else:
            mod, = _ada_mod(c_pad, ada_w[l], ada_b[l])
        mod3 = mod[:batch].reshape(batch, 1, N_MOD * d)

        h = _norm_mod(x2, norm_ffn1_g[l], mod3, 0, 1, seq)
        act, (w2_b, w_in_b, w_out_b, w1_b, w3_b) = _ffn_up(
            h, ffn1_w1[l], ffn1_w3[l],
            [(ffn1_w2[l], 32), (w_in[l], 16), (w_out[l], 16), (ffn2_w1[l], 16), (ffn2_w3[l], 16)], n_tiles=1)
        x2, xb = _matmul_residual([act], [(w2_b, 0, d_ff)], x2, mod3, 2, 0.5, seq, 512, 512, (F32, BF16))

        h = _norm_mod(xb, norm_mix_g[l], mod3, 3, 4, seq)
        proj = _proj(h, w_in_b, cos, sin, sgu_norm_g[l].reshape(1, sgu_width), sgu_norm_b[l].reshape(1, sgu_width),
                     ret_width, sgu_width, dh, dg)
        ret = _retention(proj, ret_width, decay, xi, zeta, cd, batch, seq)
        bias = jnp.broadcast_to(sgu_b_s[l][:, :, None], (SGU_GROUPS, CHUNK, dg))
        gated = _sgu(proj, 4 * ret_width, sgu_width, sgu_w_s[l], bias)
        x2, xb = _matmul_residual([ret, gated], [(w_out_b, 0, ret_width), (w_out_b, 1, sgu_width)], x2, mod3, 5, 1.0,
                                  seq, 1024, 1024, (F32, BF16))

        h = _norm_mod(xb, norm_ffn2_g[l], mod3, 6, 7, seq)
        act, (w2_b,) = _ffn_up(h, w1_b, w3_b, [(ffn2_w2[l], 128)], n_tiles=2)
        last = l == depth - 1
        outs = _matmul_residual([act], [(w2_b, 0, d_ff)], x2, mod3, 8, 0.5, seq, 512, 512,
                                (BF16,) if last else (F32, BF16))
        x2, xb = (None, outs[0]) if last else outs

    return _final_norm(xb, final_norm_g).reshape(batch, seq, d)
```

```python
import functools

import jax
import jax.numpy as jnp
from jax import lax
from jax.experimental import pallas as pl
from jax.experimental.pallas import tpu as pltpu

F32 = jnp.float32
BF16 = jnp.bfloat16

RET_HEADS = 8
SGU_GROUPS = 8
CHUNK = 128
RET_CHUNK = 256
NORM_SLAB = 8
NORM_UNROLL = 8
PROJ_SUB_TILES = 4
ROPE_BASE = 10000.0
GELU_C = 0.7978845608028654
GELU_A = 0.044715
EPS = 1e-6
N_MOD = 9

V7X_VMEM_BYTES = 64 * 1024 * 1024
VMEM_LIMIT_BYTES = V7X_VMEM_BYTES - 3 * 1024 * 1024


def _params(*semantics):
    return pltpu.CompilerParams(dimension_semantics=semantics, vmem_limit_bytes=VMEM_LIMIT_BYTES)


def _silu(x):
    half = 0.5 * x
    return half + half * jnp.tanh(half)


def _gelu_tanh(x):
    half = 0.5 * x
    return half + half * jnp.tanh(x * (GELU_C + (GELU_C * GELU_A) * (x * x)))


def _ada_kernel(c_ref, w_ref, b_ref, o_ref):
    sc = _silu(c_ref[...]).astype(BF16)
    o_ref[...] = jnp.dot(sc, w_ref[...].astype(BF16), preferred_element_type=F32) + b_ref[...]


def _ada_rope_kernel(c_ref, w_ref, b_ref, pos_ref, freq_ref, o_ref, cos_ref, sin_ref):
    _ada_kernel(c_ref, w_ref, b_ref, o_ref)
    ang = pos_ref[...].astype(F32) * freq_ref[...]
    cos_ref[...] = jnp.cos(ang)
    sin_ref[...] = jnp.sin(ang)


def _ada_mod(c_pad, w, b, rope=None, tn=1024, rope_rows=512):
    rows, d = c_pad.shape
    n = w.shape[1]
    in_specs = [
        pl.BlockSpec((rows, d), lambda j: (0, 0)),
        pl.BlockSpec((d, tn), lambda j: (0, j)),
        pl.BlockSpec((1, tn), lambda j: (0, j)),
    ]
    operands = [c_pad, w, b.reshape(1, n)]
    out_specs = [pl.BlockSpec((rows, tn), lambda j: (0, j))]
    out_shape = [jax.ShapeDtypeStruct((rows, n), F32)]
    if rope is not None:
        pos_col, inv_freq = rope
        tokens, half = pos_col.shape[0], inv_freq.shape[0]
        nb = tokens // rope_rows
        assert nb * rope_rows == tokens and nb <= n // tn
        walk = lambda j: (jnp.minimum(j, nb - 1), 0)
        in_specs += [pl.BlockSpec((rope_rows, 1), walk), pl.BlockSpec((1, half), lambda j: (0, 0))]
        operands += [pos_col, inv_freq.reshape(1, half)]
        out_specs += [pl.BlockSpec((rope_rows, half), walk)] * 2
        out_shape += [jax.ShapeDtypeStruct((tokens, half), F32)] * 2
    return pl.pallas_call(
        _ada_kernel if rope is None else _ada_rope_kernel,
        grid=(n // tn,),
        in_specs=in_specs,
        out_specs=out_specs,
        out_shape=out_shape,
        compiler_params=_params("arbitrary"),
        name="ada_mod",
    )(*operands)


def _norm_rows(x_ref, o_ref, finish, slab):
    def body(s, carry):
        rows = pl.ds(pl.multiple_of(s * slab, slab), slab)
        x = x_ref[rows, :].astype(F32)
        r = lax.rsqrt(jnp.mean(x * x, axis=-1, keepdims=True) + EPS)
        o_ref[rows, :] = finish(x * r).astype(o_ref.dtype)
        return carry

    lax.fori_loop(0, x_ref.shape[0] // slab, body, 0, unroll=NORM_UNROLL)


def _norm_mod_kernel(x_ref, g_ref, sh_ref, sc_ref, o_ref, *, slab):
    gain = g_ref[...] * (1.0 + sc_ref[0])
    shift = sh_ref[0]
    _norm_rows(x_ref, o_ref, lambda y: y * gain + shift, slab)


def _norm_kernel(x_ref, g_ref, o_ref, *, slab):
    g = g_ref[...]
    _norm_rows(x_ref, o_ref, lambda y: y * g, slab)


def _norm_mod(x2, g, mod3, seg_shift, seg_scale, seq, tm=1024):
    m, d = x2.shape
    bidx = lambda i: (i * tm) // seq
    return pl.pallas_call(
        functools.partial(_norm_mod_kernel, slab=NORM_SLAB),
        grid=(m // tm,),
        in_specs=[
            pl.BlockSpec((tm, d), lambda i: (i, 0)),
            pl.BlockSpec((1, d), lambda i: (0, 0)),
            pl.BlockSpec((1, 1, d), lambda i: (bidx(i), 0, seg_shift)),
            pl.BlockSpec((1, 1, d), lambda i: (bidx(i), 0, seg_scale)),
        ],
        out_specs=pl.BlockSpec((tm, d), lambda i: (i, 0)),
        out_shape=jax.ShapeDtypeStruct((m, d), BF16),
        compiler_params=_params("parallel"),
        name="norm_mod",
    )(x2, g.reshape(1, d), mod3, mod3)


def _final_norm(x2, g, tm=1024):
    m, d = x2.shape
    return pl.pallas_call(
        functools.partial(_norm_kernel, slab=NORM_SLAB),
        grid=(m // tm,),
        in_specs=[pl.BlockSpec((tm, d), lambda i: (i, 0)), pl.BlockSpec((1, d), lambda i: (0, 0))],
        out_specs=pl.BlockSpec((tm, d), lambda i: (i, 0)),
        out_shape=jax.ShapeDtypeStruct((m, d), F32),
        compiler_params=_params("parallel"),
        name="final_norm",
    )(x2, g.reshape(1, d))


def _as_bf16(ref):
    w = ref[...]
    return w if w.dtype == BF16 else w.astype(BF16)


def _ffn_up_kernel(*refs, n_tiles, n_side, last_tile):
    h_ref = refs[0]
    w1_refs = refs[1:1 + n_tiles]
    w3_refs = refs[1 + n_tiles:1 + 2 * n_tiles]
    side_in = refs[1 + 2 * n_tiles:1 + 2 * n_tiles + n_side]
    o_ref = refs[1 + 2 * n_tiles + n_side]
    side_out = refs[2 + 2 * n_tiles + n_side:]
    tn = w1_refs[0].shape[1]

    def tile(t):
        h = h_ref[...]
        a = jnp.dot(h, _as_bf16(w1_refs[t]), preferred_element_type=F32)
        b = jnp.dot(h, _as_bf16(w3_refs[t]), preferred_element_type=F32)
        o_ref[:, t * tn:(t + 1) * tn] = (_silu(a) * b).astype(o_ref.dtype)

    if n_tiles == 1:
        tile(0)
    else:
        first = pl.program_id(1) * n_tiles

        @pl.when(first + n_tiles - 1 <= last_tile)
        def _():
            for t in range(n_tiles):
                tile(t)

        @pl.when(first + n_tiles - 1 > last_tile)
        def _():
            for t in range(n_tiles):
                if_valid = first + t <= last_tile
                pl.when(if_valid)(functools.partial(tile, t))

                @pl.when(jnp.logical_not(if_valid))
                def _():
                    o_ref[:, t * tn:(t + 1) * tn] = jnp.zeros((o_ref.shape[0], tn), o_ref.dtype)

    for src, dst in zip(side_in, side_out):
        dst[...] = src[...].astype(dst.dtype)


def _ffn_up(h, w1, w3, side, n_tiles, tm=2048, tn=256):
    m, k = h.shape
    f = w1.shape[1]
    last_tile = f // tn - 1
    steps_j = pl.cdiv(last_tile + 1, n_tiles)
    walk = lambda nb: (lambda i, j: (jnp.minimum(i * steps_j + j, nb - 1), 0))
    wspecs = [pl.BlockSpec((k, tn), lambda i, j, t=t: (0, jnp.minimum(j * n_tiles + t, last_tile)))
              for t in range(n_tiles)]
    side_specs, side_shapes = [], []
    for w, rb in side:
        nb = w.shape[0] // rb
        assert nb * rb == w.shape[0] and nb <= (m // tm) * steps_j
        side_specs.append(pl.BlockSpec((rb, w.shape[1]), walk(nb)))
        side_shapes.append(jax.ShapeDtypeStruct(w.shape, BF16))
    outs = pl.pallas_call(
        functools.partial(_ffn_up_kernel, n_tiles=n_tiles, n_side=len(side), last_tile=last_tile),
        grid=(m // tm, steps_j),
        in_specs=[pl.BlockSpec((tm, k), lambda i, j: (i, 0), pipeline_mode=pl.Buffered(1))]
        + wspecs + wspecs + side_specs,
        out_specs=[pl.BlockSpec((tm, n_tiles * tn), lambda i, j: (i, j))] + side_specs,
        out_shape=[jax.ShapeDtypeStruct((m, steps_j * n_tiles * tn), BF16)] + side_shapes,
        compiler_params=_params("arbitrary", "arbitrary"),
        name="ffn_up",
    )(h, *([w1] * n_tiles), *([w3] * n_tiles), *[w for w, _ in side])
    return outs[0], outs[1:]


def _residual_kernel(*refs, n_pairs, scale):
    a_refs = refs[:n_pairs]
    b_refs = refs[n_pairs:2 * n_pairs]
    x_ref, gate_ref = refs[2 * n_pairs:2 * n_pairs + 2]
    o_refs = refs[2 * n_pairs + 2:]
    acc = None
    for a_ref, b_ref in zip(a_refs, b_refs):
        part = jnp.dot(a_ref[:, :b_ref.shape[0]], b_ref[...], preferred_element_type=F32)
        acc = part if acc is None else acc + part
    y = x_ref[...] + (scale * gate_ref[0]) * acc
    for o_ref in o_refs:
        o_ref[...] = y.astype(o_ref.dtype)


def _matmul_residual(a_list, b_list, x2, mod3, seg_gate, scale, seq, tm, tn, out_dtypes):
    m, n = x2.shape
    n_pairs = len(a_list)
    in_specs = []
    for a in a_list:
        in_specs.append(pl.BlockSpec((tm, a.shape[1]), lambda j, i: (i, 0)))
    operands = list(a_list)
    for w, row_block, rows in b_list:
        in_specs.append(pl.BlockSpec((rows, tn), lambda j, i, rb=row_block: (rb, j)))
        operands.append(w)
    in_specs.append(pl.BlockSpec((tm, tn), lambda j, i: (i, j)))
    in_specs.append(pl.BlockSpec((1, 1, tn), lambda j, i: ((i * tm) // seq, 0, seg_gate * (n // tn) + j)))
    return pl.pallas_call(
        functools.partial(_residual_kernel, n_pairs=n_pairs, scale=scale),
        grid=(n // tn, m // tm),
        in_specs=in_specs,
        out_specs=[pl.BlockSpec((tm, tn), lambda j, i: (i, j))] * len(out_dtypes),
        out_shape=[jax.ShapeDtypeStruct((m, n), dt) for dt in out_dtypes],
        compiler_params=_params("parallel", "parallel"),
        name="matmul_residual",
    )(*operands, x2, mod3)


def _proj_kernel(h_ref, w_ref, cos_ref, sin_ref, gn_g_ref, gn_b_ref, o_ref, *, ends, head_dim, group_dim, k_scale):
    j = pl.program_id(1)
    q_end, k_end, v_end, g_end, u_end = ends
    tn = w_ref.shape[1]

    def run(epilogue):
        sub = h_ref.shape[0] // PROJ_SUB_TILES
        for r in range(PROJ_SUB_TILES):
            rows = slice(r * sub, (r + 1) * sub)
            acc = jnp.dot(h_ref[rows, :], w_ref[...], preferred_element_type=F32)
            epilogue(acc, rows)

    def rotary(acc, rows):
        scale = jnp.where(j >= q_end, k_scale, 1.0).astype(F32)
        cos = cos_ref[rows, :] * scale
        sin = sin_ref[rows, :] * scale
        half = head_dim // 2
        for hd in range(tn // head_dim):
            lo = slice(hd * head_dim, hd * head_dim + half)
            hi = slice(hd * head_dim + half, (hd + 1) * head_dim)
            t1, t2 = acc[:, lo], acc[:, hi]
            o_ref[rows, lo] = (t1 * cos - t2 * sin).astype(o_ref.dtype)
            o_ref[rows, hi] = (t1 * sin + t2 * cos).astype(o_ref.dtype)

    def pointwise(act):
        def epilogue(acc, rows):
            o_ref[rows, :] = act(acc).astype(o_ref.dtype)
        return epilogue

    def gelu_groupnorm(acc, rows):
        for gi in range(tn // group_dim):
            cols = slice(gi * group_dim, (gi + 1) * group_dim)
            v = _gelu_tanh(acc[:, cols])
            mu = jnp.mean(v, axis=-1, keepdims=True)
            cen = v - mu
            var = jnp.mean(cen * cen, axis=-1, keepdims=True)
            o_ref[rows, cols] = (cen * lax.rsqrt(var + EPS) * gn_g_ref[:, cols] + gn_b_ref[:, cols]).astype(
                o_ref.dtype)

    pl.when(j < k_end)(lambda: run(rotary))
    plain = ((j >= k_end) & (j < v_end)) | ((j >= g_end) & (j < u_end))
    pl.when(plain)(lambda: run(pointwise(lambda t: t)))
    pl.when((j >= v_end) & (j < g_end))(lambda: run(pointwise(_silu)))
    pl.when(j >= u_end)(lambda: run(gelu_groupnorm))


def _proj(h, w, cos, sin, gn_g, gn_b, ret_width, sgu_width, head_dim, group_dim, tm=1024, tn=1024):
    m, k = h.shape
    n = w.shape[1]
    rb, sb = ret_width // tn, sgu_width // tn
    ends = (rb, 2 * rb, 3 * rb, 4 * rb, 4 * rb + sb)
    gn_spec = pl.BlockSpec((1, tn), lambda i, j: (0, jnp.clip(j - ends[4], 0, sb - 1)))
    rope_spec = pl.BlockSpec((tm, head_dim // 2), lambda i, j: (i, 0))
    return pl.pallas_call(
        functools.partial(_proj_kernel, ends=ends, head_dim=head_dim, group_dim=group_dim,
                          k_scale=head_dim ** -0.5),
        grid=(m // tm, n // tn),
        in_specs=[pl.BlockSpec((tm, k), lambda i, j: (i, 0)), pl.BlockSpec((k, tn), lambda i, j: (0, j)),
                  rope_spec, rope_spec, gn_spec, gn_spec],
        out_specs=pl.BlockSpec((tm, tn), lambda i, j: (i, j)),
        out_shape=jax.ShapeDtypeStruct((m, n), BF16),
        compiler_params=_params("parallel", "arbitrary"),
        name="proj",
    )(h, w, cos, sin, gn_g, gn_b)


def _retention_kernel(q_ref, k_ref, v_ref, g_ref, decay_ref, xi_ref, zeta_ref, cd_ref, o_ref, state_ref,
                      *, n_chunks, chunk):
    @pl.when(pl.program_id(1) == 0)
    def _():
        state_ref[...] = jnp.zeros_like(state_ref)

    dh = state_ref.shape[1]
    for h in range(RET_HEADS):
        cols = slice(h * dh, (h + 1) * dh)
        for c in range(n_chunks):
            rows = slice(c * chunk, (c + 1) * chunk)
            q = q_ref[rows, cols]
            k = k_ref[rows, cols]
            v = v_ref[rows, cols]
            state = state_ref[h]
            scores = lax.dot_general(q, k, (((1,), (1,)), ((), ())), preferred_element_type=F32) * decay_ref[h]
            intra = jnp.dot(scores.astype(BF16), v, preferred_element_type=F32)
            inter = jnp.dot(q, state.astype(BF16), preferred_element_type=F32) * xi_ref[h]
            kz = (k.astype(F32) * zeta_ref[h]).astype(BF16)
            kv = lax.dot_general(kz, v, (((0,), (0,)), ((), ())), preferred_element_type=F32)
            state_ref[h] = state * cd_ref[h] + kv
            out = intra + inter
            out = out * lax.rsqrt(jnp.mean(out * out, axis=-1, keepdims=True) + EPS)
            o_ref[rows, cols] = (out * g_ref[rows, cols].astype(F32)).astype(o_ref.dtype)


def _retention(proj, width, decay, xi, zeta, cd, batch, seq, n_chunks=4):
    m = proj.shape[0]
    dh = width // RET_HEADS
    chunk = decay.shape[1]
    rows = n_chunks * chunk
    steps = seq // rows
    tok = lambda part: pl.BlockSpec((rows, width), lambda b, c: (b * steps + c, part))
    table = lambda t: pl.BlockSpec(t.shape, lambda b, c: (0, 0, 0))
    return pl.pallas_call(
        functools.partial(_retention_kernel, n_chunks=n_chunks, chunk=chunk),
        grid=(batch, steps),
        in_specs=[tok(0), tok(1), tok(2), tok(3), table(decay), table(xi), table(zeta), table(cd)],
        out_specs=tok(0),
        out_shape=jax.ShapeDtypeStruct((m, width), BF16),
        scratch_shapes=[pltpu.VMEM((RET_HEADS, dh, dh), F32)],
        compiler_params=_params("parallel", "arbitrary"),
        name="retention",
    )(proj, proj, proj, proj, decay, xi, zeta, cd)


def _sgu_kernel(u_ref, vg_ref, w_ref, b_ref, o_ref, *, n_chunks):
    row = lax.broadcasted_iota(jnp.int32, (CHUNK, CHUNK), 0)
    col = lax.broadcasted_iota(jnp.int32, (CHUNK, CHUNK), 1)
    causal = (row >= col).astype(F32)
    dg = b_ref.shape[2]
    for g in range(SGU_GROUPS):
        cols = slice(g * dg, (g + 1) * dg)
        w = (w_ref[g] * causal).astype(BF16)
        bias = b_ref[g]
        for c in range(n_chunks):
            rows = slice(c * CHUNK, (c + 1) * CHUNK)
            mixed = jnp.dot(w, vg_ref[rows, cols], preferred_element_type=F32) + bias
            o_ref[rows, cols] = (_gelu_tanh(u_ref[rows, cols].astype(F32)) * mixed).astype(o_ref.dtype)


def _sgu(proj, col0, width, w_s, bias, n_chunks=8):
    m = proj.shape[0]
    rows = n_chunks * CHUNK
    return pl.pallas_call(
        functools.partial(_sgu_kernel, n_chunks=n_chunks),
        grid=(m // rows,),
        in_specs=[pl.BlockSpec((rows, width), lambda i: (i, col0 // width)),
                  pl.BlockSpec((rows, width), lambda i: (i, col0 // width + 1)),
                  pl.BlockSpec(w_s.shape, lambda i: (0, 0, 0)),
                  pl.BlockSpec(bias.shape, lambda i: (0, 0, 0))],
        out_specs=pl.BlockSpec((rows, width), lambda i: (i, 0)),
        out_shape=jax.ShapeDtypeStruct((m, width), BF16),
        compiler_params=_params("parallel"),
        name="sgu",
    )(proj, proj, w_s, bias)


def _retention_tables(dh, chunk):
    log_gamma = jnp.log(1.0 - 2.0 ** (-5.0 - jnp.arange(RET_HEADS, dtype=F32)))
    idx = jnp.arange(chunk)
    dist = (idx[:, None] - idx[None, :]).astype(F32)
    decay = jnp.where(dist[None] >= 0, jnp.exp(log_gamma[:, None, None] * jnp.maximum(dist, 0.0)[None]), 0.0)
    xi = jnp.exp(log_gamma[:, None] * (idx + 1).astype(F32))
    zeta = jnp.exp(log_gamma[:, None] * (chunk - 1 - idx).astype(F32))
    cd = jnp.exp(log_gamma * chunk)
    bcast = lambda t: jnp.broadcast_to(t[:, :, None], (RET_HEADS, chunk, dh))
    return decay, bcast(xi), bcast(zeta), jnp.broadcast_to(cd[:, None, None], (RET_HEADS, 1, dh))


def kernel(x, c, positions, ada_w, ada_b, norm_ffn1_g, ffn1_w1, ffn1_w3, ffn1_w2, norm_mix_g, w_in, sgu_norm_g, sgu_norm_b, sgu_w_s, sgu_b_s, w_out, norm_ffn2_g, ffn2_w1, ffn2_w3, ffn2_w2, final_norm_g):
    batch, seq, d = x.shape
    m = batch * seq
    depth = ada_w.shape[0]
    d_ff = ffn1_w2.shape[1]
    ret_width = w_out.shape[1] // 2
    sgu_width = w_out.shape[1] - ret_width
    dh = ret_width // RET_HEADS
    dg = sgu_width // SGU_GROUPS

    x2 = x.reshape(m, d)
    c_pad = jnp.zeros((8, d), F32).at[:batch].set(c)
    inv_freq = ROPE_BASE ** (-jnp.arange(0, dh // 2, dtype=F32) / (dh // 2))
    decay, xi, zeta, cd = _retention_tables(dh, RET_CHUNK)

    for l in range(depth):
        if l == 0:
            mod, cos, sin = _ada_mod(c_pad, ada_w[l], ada_b[l], rope=(positions.reshape(m, 1), inv_freq))
        else:
            mod, = _ada_mod(c_pad, ada_w[l], ada_b[l])
        mod3 = mod[:batch].reshape(batch, 1, N_MOD * d)

        h = _norm_mod(x2, norm_ffn1_g[l], mod3, 0, 1, seq)
        act, (w2_b, w_in_b, w_out_b, w1_b, w3_b) = _ffn_up(
            h, ffn1_w1[l], ffn1_w3[l],
            [(ffn1_w2[l], 32), (w_in[l], 16), (w_out[l], 16), (ffn2_w1[l], 16), (ffn2_w3[l], 16)], n_tiles=1)
        x2, xb = _matmul_residual([act], [(w2_b, 0, d_ff)], x2, mod3, 2, 0.5, seq, 512, 512, (F32, BF16))

        h = _norm_mod(xb, norm_mix_g[l], mod3, 3, 4, seq)
        proj = _proj(h, w_in_b, cos, sin, sgu_norm_g[l].reshape(1, sgu_width), sgu_norm_b[l].reshape(1, sgu_width),
                     ret_width, sgu_width, dh, dg)
        ret = _retention(proj, ret_width, decay, xi, zeta, cd, batch, seq)
        bias = jnp.broadcast_to(sgu_b_s[l][:, :, None], (SGU_GROUPS, CHUNK, dg))
        gated = _sgu(proj, 4 * ret_width, sgu_width, sgu_w_s[l], bias)
        x2, xb = _matmul_residual([ret, gated], [(w_out_b, 0, ret_width), (w_out_b, 1, sgu_width)], x2, mod3, 5, 1.0,
                                  seq, 1024, 1024, (F32, BF16))

        h = _norm_mod(xb, norm_ffn2_g[l], mod3, 6, 7, seq)
        act, (w2_b,) = _ffn_up(h, w1_b, w3_b, [(ffn2_w2[l], 128)], n_tiles=2)
        last = l == depth - 1
        outs = _matmul_residual([act], [(w2_b, 0, d_ff)], x2, mod3, 8, 0.5, seq, 512, 512,
                                (BF16,) if last else (F32, BF16))
        x2, xb = (None, outs[0]) if last else outs

    return _final_norm(xb, final_norm_g).reshape(batch, seq, d)
```

```python
import functools

import jax
import jax.numpy as jnp
from jax import lax
from jax.experimental import pallas as pl
from jax.experimental.pallas import tpu as pltpu

F32 = jnp.float32
BF16 = jnp.bfloat16

RET_HEADS = 8
SGU_GROUPS = 8
CHUNK = 128
RET_CHUNK = 256
NORM_SLAB = 8
NORM_UNROLL = 8
PROJ_SUB_TILES = 2
ROPE_BASE = 10000.0
GELU_C = 0.7978845608028654
GELU_A = 0.044715
EPS = 1e-6
N_MOD = 9

V7X_VMEM_BYTES = 64 * 1024 * 1024
VMEM_LIMIT_BYTES = V7X_VMEM_BYTES - 3 * 1024 * 1024


def _params(*semantics):
    return pltpu.CompilerParams(dimension_semantics=semantics, vmem_limit_bytes=VMEM_LIMIT_BYTES)


def _silu(x):
    half = 0.5 * x
    return half + half * jnp.tanh(half)


def _gelu_tanh(x):
    half = 0.5 * x
    return half + half * jnp.tanh(x * (GELU_C + (GELU_C * GELU_A) * (x * x)))


def _ada_kernel(c_ref, w_ref, b_ref, o_ref):
    sc = _silu(c_ref[...]).astype(BF16)
    o_ref[...] = jnp.dot(sc, w_ref[...].astype(BF16), preferred_element_type=F32) + b_ref[...]


def _ada_rope_kernel(c_ref, w_ref, b_ref, pos_ref, freq_ref, o_ref, cos_ref, sin_ref):
    _ada_kernel(c_ref, w_ref, b_ref, o_ref)
    ang = pos_ref[...].astype(F32) * freq_ref[...]
    cos_ref[...] = jnp.cos(ang)
    sin_ref[...] = jnp.sin(ang)


def _ada_mod(c_pad, w, b, rope=None, tn=512, rope_rows=256):
    rows, d = c_pad.shape
    n = w.shape[1]
    in_specs = [
        pl.BlockSpec((rows, d), lambda j: (0, 0)),
        pl.BlockSpec((d, tn), lambda j: (0, j)),
        pl.BlockSpec((1, tn), lambda j: (0, j)),
    ]
    operands = [c_pad, w, b.reshape(1, n)]
    out_specs = [pl.BlockSpec((rows, tn), lambda j: (0, j))]
    out_shape = [jax.ShapeDtypeStruct((rows, n), F32)]
    if rope is not None:
        pos_col, inv_freq = rope
        tokens, half = pos_col.shape[0], inv_freq.shape[0]
        nb = tokens // rope_rows
        assert nb * rope_rows == tokens and nb <= n // tn
        walk = lambda j: (jnp.minimum(j, nb - 1), 0)
        in_specs += [pl.BlockSpec((rope_rows, 1), walk), pl.BlockSpec((1, half), lambda j: (0, 0))]
        operands += [pos_col, inv_freq.reshape(1, half)]
        out_specs += [pl.BlockSpec((rope_rows, half), walk)] * 2
        out_shape += [jax.ShapeDtypeStruct((tokens, half), F32)] * 2
    return pl.pallas_call(
        _ada_kernel if rope is None else _ada_rope_kernel,
        grid=(n // tn,),
        in_specs=in_specs,
        out_specs=out_specs,
        out_shape=out_shape,
        compiler_params=_params("arbitrary"),
        name="ada_mod",
    )(*operands)


def _norm_rows(x_ref, o_ref, finish, slab):
    def body(s, carry):
        rows = pl.ds(pl.multiple_of(s * slab, slab), slab)
        x = x_ref[rows, :].astype(F32)
        r = lax.rsqrt(jnp.mean(x * x, axis=-1, keepdims=True) + EPS)
        o_ref[rows, :] = finish(x * r).astype(o_ref.dtype)
        return carry

    lax.fori_loop(0, x_ref.shape[0] // slab, body, 0, unroll=NORM_UNROLL)


def _norm_mod_kernel(x_ref, g_ref, sh_ref, sc_ref, o_ref, *, slab):
    gain = g_ref[...] * (1.0 + sc_ref[0])
    shift = sh_ref[0]
    _norm_rows(x_ref, o_ref, lambda y: y * gain + shift, slab)


def _norm_kernel(x_ref, g_ref, o_ref, *, slab):
    g = g_ref[...]
    _norm_rows(x_ref, o_ref, lambda y: y * g, slab)


def _norm_mod(x2, g, mod3, seg_shift, seg_scale, seq, tm=1024):
    m, d = x2.shape
    bidx = lambda i: (i * tm) // seq
    return pl.pallas_call(
        functools.partial(_norm_mod_kernel, slab=NORM_SLAB),
        grid=(m // tm,),
        in_specs=[
            pl.BlockSpec((tm, d), lambda i: (i, 0)),
            pl.BlockSpec((1, d), lambda i: (0, 0)),
            pl.BlockSpec((1, 1, d), lambda i: (bidx(i), 0, seg_shift)),
            pl.BlockSpec((1, 1, d), lambda i: (bidx(i), 0, seg_scale)),
        ],
        out_specs=pl.BlockSpec((tm, d), lambda i: (i, 0)),
        out_shape=jax.ShapeDtypeStruct((m, d), BF16),
        compiler_params=_params("parallel"),
        name="norm_mod",
    )(x2, g.reshape(1, d), mod3, mod3)


def _final_norm(x2, g, tm=1024):
    m, d = x2.shape
    return pl.pallas_call(
        functools.partial(_norm_kernel, slab=NORM_SLAB),
        grid=(m // tm,),
        in_specs=[pl.BlockSpec((tm, d), lambda i: (i, 0)), pl.BlockSpec((1, d), lambda i: (0, 0))],
        out_specs=pl.BlockSpec((tm, d), lambda i: (i, 0)),
        out_shape=jax.ShapeDtypeStruct((m, d), F32),
        compiler_params=_params("parallel"),
        name="final_norm",
    )(x2, g.reshape(1, d))


def _as_bf16(ref):
    w = ref[...]
    return w if w.dtype == BF16 else w.astype(BF16)


def _ffn_up_kernel(*refs, n_tiles, n_side, last_tile):
    h_ref = refs[0]
    w1_refs = refs[1:1 + n_tiles]
    w3_refs = refs[1 + n_tiles:1 + 2 * n_tiles]
    side_in = refs[1 + 2 * n_tiles:1 + 2 * n_tiles + n_side]
    o_ref = refs[1 + 2 * n_tiles + n_side]
    side_out = refs[2 + 2 * n_tiles + n_side:]
    tn = w1_refs[0].shape[1]

    def tile(t):
        h = h_ref[...]
        a = jnp.dot(h, _as_bf16(w1_refs[t]), preferred_element_type=F32)
        b = jnp.dot(h, _as_bf16(w3_refs[t]), preferred_element_type=F32)
        o_ref[:, t * tn:(t + 1) * tn] = (_silu(a) * b).astype(o_ref.dtype)

    if n_tiles == 1:
        tile(0)
    else:
        first = pl.program_id(1) * n_tiles

        @pl.when(first + n_tiles - 1 <= last_tile)
        def _():
            for t in range(n_tiles):
                tile(t)

        @pl.when(first + n_tiles - 1 > last_tile)
        def _():
            for t in range(n_tiles):
                if_valid = first + t <= last_tile
                pl.when(if_valid)(functools.partial(tile, t))

                @pl.when(jnp.logical_not(if_valid))
                def _():
                    o_ref[:, t * tn:(t + 1) * tn] = jnp.zeros((o_ref.shape[0], tn), o_ref.dtype)

    for src, dst in zip(side_in, side_out):
        dst[...] = src[...].astype(dst.dtype)


def _ffn_up(h, w1, w3, side, n_tiles, tm=2048, tn=256):
    m, k = h.shape
    f = w1.shape[1]
    last_tile = f // tn - 1
    steps_j = pl.cdiv(last_tile + 1, n_tiles)
    walk = lambda nb: (lambda i, j: (jnp.minimum(i * steps_j + j, nb - 1), 0))
    wspecs = [pl.BlockSpec((k, tn), lambda i, j, t=t: (0, jnp.minimum(j * n_tiles + t, last_tile)))
              for t in range(n_tiles)]
    side_specs, side_shapes = [], []
    for w, rb in side:
        nb = w.shape[0] // rb
        assert nb * rb == w.shape[0] and nb <= (m // tm) * steps_j
        side_specs.append(pl.BlockSpec((rb, w.shape[1]), walk(nb)))
        side_shapes.append(jax.ShapeDtypeStruct(w.shape, BF16))
    outs = pl.pallas_call(
        functools.partial(_ffn_up_kernel, n_tiles=n_tiles, n_side=len(side), last_tile=last_tile),
        grid=(m // tm, steps_j),
        in_specs=[pl.BlockSpec((tm, k), lambda i, j: (i, 0), pipeline_mode=pl.Buffered(1))]
        + wspecs + wspecs + side_specs,
        out_specs=[pl.BlockSpec((tm, n_tiles * tn), lambda i, j: (i, j))] + side_specs,
        out_shape=[jax.ShapeDtypeStruct((m, steps_j * n_tiles * tn), BF16)] + side_shapes,
        compiler_params=_params("arbitrary", "arbitrary"),
        name="ffn_up",
    )(h, *([w1] * n_tiles), *([w3] * n_tiles), *[w for w, _ in side])
    return outs[0], outs[1:]


def _residual_kernel(*refs, n_pairs, scale):
    a_refs = refs[:n_pairs]
    b_refs = refs[n_pairs:2 * n_pairs]
    x_ref, gate_ref = refs[2 * n_pairs:2 * n_pairs + 2]
    o_refs = refs[2 * n_pairs + 2:]
    acc = None
    for a_ref, b_ref in zip(a_refs, b_refs):
        part = jnp.dot(a_ref[:, :b_ref.shape[0]], b_ref[...], preferred_element_type=F32)
        acc = part if acc is None else acc + part
    y = x_ref[...] + (scale * gate_ref[0]) * acc
    for o_ref in o_refs:
        o_ref[...] = y.astype(o_ref.dtype)


def _matmul_residual(a_list, b_list, x2, mod3, seg_gate, scale, seq, tm, tn, out_dtypes):
    m, n = x2.shape
    n_pairs = len(a_list)
    in_specs = []
    for a in a_list:
        in_specs.append(pl.BlockSpec((tm, a.shape[1]), lambda j, i: (i, 0)))
    operands = list(a_list)
    for w, row_block, rows in b_list:
        in_specs.append(pl.BlockSpec((rows, tn), lambda j, i, rb=row_block: (rb, j)))
        operands.append(w)
    in_specs.append(pl.BlockSpec((tm, tn), lambda j, i: (i, j)))
    in_specs.append(pl.BlockSpec((1, 1, tn), lambda j, i: ((i * tm) // seq, 0, seg_gate * (n // tn) + j)))
    return pl.pallas_call(
        functools.partial(_residual_kernel, n_pairs=n_pairs, scale=scale),
        grid=(n // tn, m // tm),
        in_specs=in_specs,
        out_specs=[pl.BlockSpec((tm, tn), lambda j, i: (i, j))] * len(out_dtypes),
        out_shape=[jax.ShapeDtypeStruct((m, n), dt) for dt in out_dtypes],
        compiler_params=_params("parallel", "parallel"),
        name="matmul_residual",
    )(*operands, x2, mod3)


def _proj_kernel(h_ref, w_ref, cos_ref, sin_ref, gn_g_ref, gn_b_ref, o_ref, *, ends, head_dim, group_dim, k_scale):
    j = pl.program_id(1)
    q_end, k_end, v_end, g_end, u_end = ends
    tn = w_ref.shape[1]

    def run(epilogue):
        sub = h_ref.shape[0] // PROJ_SUB_TILES
        for r in range(PROJ_SUB_TILES):
            rows = slice(r * sub, (r + 1) * sub)
            acc = jnp.dot(h_ref[rows, :], w_ref[...], preferred_element_type=F32)
            epilogue(acc, rows)

    def rotary(acc, rows):
        scale = jnp.where(j >= q_end, k_scale, 1.0).astype(F32)
        cos = cos_ref[rows, :] * scale
        sin = sin_ref[rows, :] * scale
        half = head_dim // 2
        for hd in range(tn // head_dim):
            lo = slice(hd * head_dim, hd * head_dim + half)
            hi = slice(hd * head_dim + half, (hd + 1) * head_dim)
            t1, t2 = acc[:, lo], acc[:, hi]
            o_ref[rows, lo] = (t1 * cos - t2 * sin).astype(o_ref.dtype)
            o_ref[rows, hi] = (t1 * sin + t2 * cos).astype(o_ref.dtype)

    def pointwise(act):
        def epilogue(acc, rows):
            o_ref[rows, :] = act(acc).astype(o_ref.dtype)
        return epilogue

    def gelu_groupnorm(acc, rows):
        for gi in range(tn // group_dim):
            cols = slice(gi * group_dim, (gi + 1) * group_dim)
            v = _gelu_tanh(acc[:, cols])
            mu = jnp.mean(v, axis=-1, keepdims=True)
            cen = v - mu
            var = jnp.mean(cen * cen, axis=-1, keepdims=True)
            o_ref[rows, cols] = (cen * lax.rsqrt(var + EPS) * gn_g_ref[:, cols] + gn_b_ref[:, cols]).astype(
                o_ref.dtype)

    pl.when(j < k_end)(lambda: run(rotary))
    plain = ((j >= k_end) & (j < v_end)) | ((j >= g_end) & (j < u_end))
    pl.when(plain)(lambda: run(pointwise(lambda t: t)))
    pl.when((j >= v_end) & (j < g_end))(lambda: run(pointwise(_silu)))
    pl.when(j >= u_end)(lambda: run(gelu_groupnorm))


def _proj(h, w, cos, sin, gn_g, gn_b, ret_width, sgu_width, head_dim, group_dim, tm=1024, tn=1024):
    m, k = h.shape
    n = w.shape[1]
    rb, sb = ret_width // tn, sgu_width // tn
    ends = (rb, 2 * rb, 3 * rb, 4 * rb, 4 * rb + sb)
    gn_spec = pl.BlockSpec((1, tn), lambda i, j: (0, jnp.clip(j - ends[4], 0, sb - 1)))
    rope_spec = pl.BlockSpec((tm, head_dim // 2), lambda i, j: (i, 0))
    return pl.pallas_call(
        functools.partial(_proj_kernel, ends=ends, head_dim=head_dim, group_dim=group_dim,
                          k_scale=head_dim ** -0.5),
        grid=(m // tm, n // tn),
        in_specs=[pl.BlockSpec((tm, k), lambda i, j: (i, 0)), pl.BlockSpec((k, tn), lambda i, j: (0, j)),
                  rope_spec, rope_spec, gn_spec, gn_spec],
        out_specs=pl.BlockSpec((tm, tn), lambda i, j: (i, j)),
        out_shape=jax.ShapeDtypeStruct((m, n), BF16),
        compiler_params=_params("parallel", "arbitrary"),
        name="proj",
    )(h, w, cos, sin, gn_g, gn_b)


def _retention_kernel(q_ref, k_ref, v_ref, g_ref, decay_ref, xi_ref, zeta_ref, cd_ref, o_ref, state_ref,
                      *, n_chunks, chunk):
    @pl.when(pl.program_id(1) == 0)
    def _():
        state_ref[...] = jnp.zeros_like(state_ref)

    dh = state_ref.shape[1]
    for h in range(RET_HEADS):
        cols = slice(h * dh, (h + 1) * dh)
        for c in range(n_chunks):
            rows = slice(c * chunk, (c + 1) * chunk)
            q = q_ref[rows, cols]
            k = k_ref[rows, cols]
            v = v_ref[rows, cols]
            state = state_ref[h]
            scores = lax.dot_general(q, k, (((1,), (1,)), ((), ())), preferred_element_type=F32) * decay_ref[h]
            intra = jnp.dot(scores.astype(BF16), v, preferred_element_type=F32)
            inter = jnp.dot(q, state.astype(BF16), preferred_element_type=F32) * xi_ref[h]
            kz = (k.astype(F32) * zeta_ref[h]).astype(BF16)
            kv = lax.dot_general(kz, v, (((0,), (0,)), ((), ())), preferred_element_type=F32)
            state_ref[h] = state * cd_ref[h] + kv
            out = intra + inter
            out = out * lax.rsqrt(jnp.mean(out * out, axis=-1, keepdims=True) + EPS)
            o_ref[rows, cols] = (out * g_ref[rows, cols].astype(F32)).astype(o_ref.dtype)


def _mixers_kernel(q_ref, k_ref, v_ref, g_ref, decay_ref, xi_ref, zeta_ref, cd_ref, u_ref, vg_ref, w_ref, b_ref,
                   ret_ref, gated_ref, state_ref, *, ret_chunks, chunk, sgu_chunks):
    _retention_kernel(q_ref, k_ref, v_ref, g_ref, decay_ref, xi_ref, zeta_ref, cd_ref, ret_ref, state_ref,
                      n_chunks=ret_chunks, chunk=chunk)
    _sgu_kernel(u_ref, vg_ref, w_ref, b_ref, gated_ref, n_chunks=sgu_chunks)


def _mixers(proj, ret_width, sgu_width, decay, xi, zeta, cd, w_s, bias, batch, seq, rows=512):
    assert ret_width == sgu_width
    m = proj.shape[0]
    dh = ret_width // RET_HEADS
    chunk = decay.shape[1]
    steps = seq // rows
    tok = lambda part: pl.BlockSpec((rows, ret_width), lambda b, c: (b * steps + c, part))
    table = lambda t: pl.BlockSpec(t.shape, lambda b, c: (0, 0, 0))
    return pl.pallas_call(
        functools.partial(_mixers_kernel, ret_chunks=rows // chunk, chunk=chunk, sgu_chunks=rows // CHUNK),
        grid=(batch, steps),
        in_specs=[tok(0), tok(1), tok(2), tok(3), table(decay), table(xi), table(zeta), table(cd),
                  tok(4), tok(5), table(w_s), table(bias)],
        out_specs=[tok(0), tok(0)],
        out_shape=[jax.ShapeDtypeStruct((m, ret_width), BF16), jax.ShapeDtypeStruct((m, sgu_width), BF16)],
        scratch_shapes=[pltpu.VMEM((RET_HEADS, dh, dh), F32)],
        compiler_params=_params("parallel", "arbitrary"),
        name="mixers",
    )(proj, proj, proj, proj, decay, xi, zeta, cd, proj, proj, w_s, bias)


def _sgu_kernel(u_ref, vg_ref, w_ref, b_ref, o_ref, *, n_chunks):
    row = lax.broadcasted_iota(jnp.int32, (CHUNK, CHUNK), 0)
    col = lax.broadcasted_iota(jnp.int32, (CHUNK, CHUNK), 1)
    causal = (row >= col).astype(F32)
    dg = b_ref.shape[2]
    for g in range(SGU_GROUPS):
        cols = slice(g * dg, (g + 1) * dg)
        w = (w_ref[g] * causal).astype(BF16)
        bias = b_ref[g]
        for c in range(n_chunks):
            rows = slice(c * CHUNK, (c + 1) * CHUNK)
            mixed = jnp.dot(w, vg_ref[rows, cols], preferred_element_type=F32) + bias
            o_ref[rows, cols] = (_gelu_tanh(u_ref[rows, cols].astype(F32)) * mixed).astype(o_ref.dtype)


def _retention_tables(dh, chunk):
    log_gamma = jnp.log(1.0 - 2.0 ** (-5.0 - jnp.arange(RET_HEADS, dtype=F32)))
    idx = jnp.arange(chunk)
    dist = (idx[:, None] - idx[None, :]).astype(F32)
    decay = jnp.where(dist[None] >= 0, jnp.exp(log_gamma[:, None, None] * jnp.maximum(dist, 0.0)[None]), 0.0)
    xi = jnp.exp(log_gamma[:, None] * (idx + 1).astype(F32))
    zeta = jnp.exp(log_gamma[:, None] * (chunk - 1 - idx).astype(F32))
    cd = jnp.exp(log_gamma * chunk)
    bcast = lambda t: jnp.broadcast_to(t[:, :, None], (RET_HEADS, chunk, dh))
    return decay, bcast(xi), bcast(zeta), jnp.broadcast_to(cd[:, None, None], (RET_HEADS, 1, dh))


def kernel(x, c, positions, ada_w, ada_b, norm_ffn1_g, ffn1_w1, ffn1_w3, ffn1_w2, norm_mix_g, w_in, sgu_norm_g, sgu_norm_b, sgu_w_s, sgu_b_s, w_out, norm_ffn2_g, ffn2_w1, ffn2_w3, ffn2_w2, final_norm_g):
    batch, seq, d = x.shape
    m = batch * seq
    depth = ada_w.shape[0]
    d_ff = ffn1_w2.shape[1]
    ret_width = w_out.shape[1] // 2
    sgu_width = w_out.shape[1] - ret_width
    dh = ret_width // RET_HEADS
    dg = sgu_width // SGU_GROUPS

    x2 = x.reshape(m, d)
    c_pad = jnp.zeros((8, d), F32).at[:batch].set(c)
    inv_freq = ROPE_BASE ** (-jnp.arange(0, dh // 2, dtype=F32) / (dh // 2))
    decay, xi, zeta, cd = _retention_tables(dh, RET_CHUNK)

    for l in range(depth):
        if l == 0:
            mod, cos, sin = _ada_mod(c_pad, ada_w[l], ada_b[l], rope=(positions.reshape(m, 1), inv_freq))
        else:
            mod, = _ada_mod(c_pad, ada_w[l], ada_b[l])
        mod3 = mod[:batch].reshape(batch, 1, N_MOD * d)

        h = _norm_mod(x2, norm_ffn1_g[l], mod3, 0, 1, seq)
        act, (w2_b, w_in_b, w_out_b, w1_b, w3_b) = _ffn_up(
            h, ffn1_w1[l], ffn1_w3[l],
            [(ffn1_w2[l], 32), (w_in[l], 16), (w_out[l], 16), (ffn2_w1[l], 16), (ffn2_w3[l], 16)], n_tiles=1)
        x2, xb = _matmul_residual([act], [(w2_b, 0, d_ff)], x2, mod3, 2, 0.5, seq, 512, 512, (F32, BF16))

        h = _norm_mod(xb, norm_mix_g[l], mod3, 3, 4, seq)
        proj = _proj(h, w_in_b, cos, sin, sgu_norm_g[l].reshape(1, sgu_width), sgu_norm_b[l].reshape(1, sgu_width),
                     ret_width, sgu_width, dh, dg)
        bias = jnp.broadcast_to(sgu_b_s[l][:, :, None], (SGU_GROUPS, CHUNK, dg))
        ret, gated = _mixers(proj, ret_width, sgu_width, decay, xi, zeta, cd, sgu_w_s[l], bias, batch, seq)
        x2, xb = _matmul_residual([ret, gated], [(w_out_b, 0, ret_width), (w_out_b, 1, sgu_width)], x2, mod3, 5, 1.0,
                                  seq, 1024, 1024, (F32, BF16))

        h = _norm_mod(xb, norm_ffn2_g[l], mod3, 6, 7, seq)
        act, (w2_b,) = _ffn_up(h, w1_b, w3_b, [(ffn2_w2[l], 128)], n_tiles=2)
        last = l == depth - 1
        outs = _matmul_residual([act], [(w2_b, 0, d_ff)], x2, mod3, 8, 0.5, seq, 512, 512,
                                (BF16,) if last else (F32, BF16))
        x2, xb = (None, outs[0]) if last else outs

    return _final_norm(xb, final_norm_g).reshape(batch, seq, d)
```
